```python
import math
import jax, jax.numpy as jnp
from jax import lax
import numpy as np

D_MODEL = 2048
BATCH = 8
SEQ = 2048
DEPTH = 2

N_MIXERS = 2
N_CONF = (DEPTH + 1) // 2
N_HYENA = DEPTH // 2
D_FF = ((8 * D_MODEL // 3 + 255) // 256) * 256
CONV_WIDTH = 31
HYENA_SHORT_WIDTH = 3
HYENA_N_BANDS = 16
HYENA_EMB_DIM = 1 + 2 * HYENA_N_BANDS
HYENA_FILTER_ORDER = 64
HYENA_FAST_DECAY_PCT = 0.3
HYENA_SLOW_DECAY_PCT = 1.5
HYENA_DECAY_TARGET = 1e-2
NORM_EPS = 1e-6
LN_EPS = 1e-5

kernel_name = "hybrid_conformer_hyena_encoder"


def rmsnorm(x, g):
    xf = x.astype(jnp.float32)
    y = xf * lax.rsqrt(jnp.mean(xf * xf, axis=-1, keepdims=True) + NORM_EPS)
    return (y * g.astype(jnp.float32)).astype(x.dtype)


def layernorm(x, g, b):
    xf = x.astype(jnp.float32)
    mu = jnp.mean(xf, axis=-1, keepdims=True)
    var = jnp.mean(jnp.square(xf - mu), axis=-1, keepdims=True)
    y = (xf - mu) * lax.rsqrt(var + LN_EPS)
    return (y * g.astype(jnp.float32) + b.astype(jnp.float32)).astype(x.dtype)


def depthwise_conv_centred(x, w):
    k = w.shape[0]
    pad = (k - 1) // 2
    return lax.conv_general_dilated(
        x, w[:, None, :].astype(x.dtype), window_strides=(1,),
        padding=[(pad, pad)], dimension_numbers=("NWC", "WIO", "NWC"),
        feature_group_count=x.shape[-1])


def conformer_conv(h, w_pw1, b_pw1, w_dw, b_dw, ln_g, ln_b, w_pw2, b_pw2):
    a = h @ w_pw1 + b_pw1
    u = jax.nn.glu(a, axis=-1)
    u = depthwise_conv_centred(u, w_dw) + b_dw
    u = jax.nn.silu(layernorm(u, ln_g, ln_b))
    return u @ w_pw2 + b_pw2


def hyena_filters(L, w1, b1, f1, w2, b2, f2, w3, b3, f3, w4):
    f32 = jnp.float32
    t = jnp.linspace(0.0, 1.0, L, dtype=f32)[:, None]
    bands = jnp.linspace(1e-4, HYENA_N_BANDS - 1, HYENA_N_BANDS, dtype=f32)[None, :]
    wpos = (2.0 * math.pi) * jnp.arange(L, dtype=f32)[:, None] / L
    z = jnp.concatenate([t, jnp.cos(bands * wpos), -jnp.sin(bands * wpos)], axis=-1)
    hf = jnp.sin(f1.astype(f32) * (z @ w1.astype(f32) + b1.astype(f32)))
    hf = jnp.sin(f2.astype(f32) * (hf @ w2.astype(f32) + b2.astype(f32)))
    hf = jnp.sin(f3.astype(f32) * (hf @ w3.astype(f32) + b3.astype(f32)))
    k = hf @ w4.astype(f32)
    max_decay = math.log(HYENA_DECAY_TARGET) / HYENA_FAST_DECAY_PCT
    min_decay = math.log(HYENA_DECAY_TARGET) / HYENA_SLOW_DECAY_PCT
    deltas = jnp.abs(jnp.linspace(min_decay, max_decay, D_MODEL, dtype=f32))[None, :]
    decay = jnp.exp(-t * deltas)
    return k * jnp.concatenate([decay, decay], axis=-1)


def bidirectional_fftconv(v, k, skip):
    L = v.shape[1]
    kf, kb = k[:, :D_MODEL], k[:, D_MODEL:]
    kern = jnp.concatenate([kf, jnp.zeros((1, D_MODEL), jnp.float32), kb[1:][::-1]], axis=0)
    kfreq = jnp.fft.rfft(kern, axis=0)
    vf = v.astype(jnp.float32)
    vfreq = jnp.fft.rfft(vf, n=2 * L, axis=1)
    y = jnp.fft.irfft(vfreq * kfreq[None], n=2 * L, axis=1)[:, :L]
    y = y + vf * skip.astype(jnp.float32)
    return y.astype(v.dtype)


def hyena(h, w_in, b_in, w_short, b_short, w1, b1, f1, w2, b2, f2, w3, b3, f3, w4,
          skip, w_out, b_out):
    L = h.shape[1]
    zc = h @ w_in + b_in
    zc = depthwise_conv_centred(zc, w_short) + b_short
    x0, x1, v = jnp.split(zc, 3, axis=-1)
    k = hyena_filters(L, w1, b1, f1, w2, b2, f2, w3, b3, f3, w4)
    v = bidirectional_fftconv(v * x1, k, skip)
    return (v * x0) @ w_out + b_out


def swiglu(h, w_gate, w_up, w_down):
    return (jax.nn.silu(h @ w_gate) * (h @ w_up)) @ w_down


def setup_inputs(seed: int = 0) -> dict:
    key = jax.random.key(seed)
    ks = jax.random.split(key, 40)
    D, F = D_MODEL, D_FF
    f32 = jnp.float32
    nrm = lambda k, s, sc: jax.random.normal(k, s, f32) * sc
    gain = lambda k, s: 1.0 + 0.01 * jax.random.normal(k, s, f32)
    return {
        "x": jax.random.normal(ks[0], (BATCH, SEQ, D), f32),
        "norm_mix": gain(ks[1], (DEPTH, D)),
        "norm_ffn": gain(ks[2], (DEPTH, D)),
        "cv_w_pw1": nrm(ks[3], (N_CONF, D, 2 * D), D ** -0.5),
        "cv_b_pw1": nrm(ks[4], (N_CONF, 2 * D), 0.01),
        "cv_w_dw": nrm(ks[5], (N_CONF, CONV_WIDTH, D), CONV_WIDTH ** -0.5),
        "cv_b_dw": nrm(ks[6], (N_CONF, D), 0.01),
        "cv_ln_g": gain(ks[7], (N_CONF, D)),
        "cv_ln_b": nrm(ks[8], (N_CONF, D), 0.01),
        "cv_w_pw2": nrm(ks[9], (N_CONF, D, D), D ** -0.5),
        "cv_b_pw2": nrm(ks[10], (N_CONF, D), 0.01),
        "hy_w_in": nrm(ks[11], (N_HYENA, D, 3 * D), D ** -0.5),
        "hy_b_in": nrm(ks[12], (N_HYENA, 3 * D), 0.01),
        "hy_w_short": nrm(ks[13], (N_HYENA, HYENA_SHORT_WIDTH, 3 * D), HYENA_SHORT_WIDTH ** -0.5),
        "hy_b_short": nrm(ks[14], (N_HYENA, 3 * D), 0.01),
        "hy_f_w1": nrm(ks[15], (N_HYENA, HYENA_EMB_DIM, HYENA_FILTER_ORDER), HYENA_EMB_DIM ** -0.5),
        "hy_f_b1": nrm(ks[16], (N_HYENA, HYENA_FILTER_ORDER), 0.01),
        "hy_f_freq1": gain(ks[17], (N_HYENA, HYENA_FILTER_ORDER)),
        "hy_f_w2": nrm(ks[18], (N_HYENA, HYENA_FILTER_ORDER, HYENA_FILTER_ORDER), HYENA_FILTER_ORDER ** -0.5),
        "hy_f_b2": nrm(ks[19], (N_HYENA, HYENA_FILTER_ORDER), 0.01),
        "hy_f_freq2": gain(ks[20], (N_HYENA, HYENA_FILTER_ORDER)),
        "hy_f_w3": nrm(ks[21], (N_HYENA, HYENA_FILTER_ORDER, HYENA_FILTER_ORDER), HYENA_FILTER_ORDER ** -0.5),
        "hy_f_b3": nrm(ks[22], (N_HYENA, HYENA_FILTER_ORDER), 0.01),
        "hy_f_freq3": gain(ks[23], (N_HYENA, HYENA_FILTER_ORDER)),
        "hy_f_w4": nrm(ks[24], (N_HYENA, HYENA_FILTER_ORDER, 2 * D), 0.002),
        "hy_skip": nrm(ks[25], (N_HYENA, D), 1.0),
        "hy_w_out": nrm(ks[26], (N_HYENA, D, D), D ** -0.5),
        "hy_b_out": nrm(ks[27], (N_HYENA, D), 0.01),
        "ffn_w_gate": nrm(ks[28], (DEPTH, D, F), D ** -0.5),
        "ffn_w_up": nrm(ks[29], (DEPTH, D, F), D ** -0.5),
        "ffn_w_down": nrm(ks[30], (DEPTH, F, D), F ** -0.5),
        "norm_final": gain(ks[31], (D,)),
    }


def reference(x, norm_mix, norm_ffn,
              cv_w_pw1, cv_b_pw1, cv_w_dw, cv_b_dw, cv_ln_g, cv_ln_b, cv_w_pw2, cv_b_pw2,
              hy_w_in, hy_b_in, hy_w_short, hy_b_short,
              hy_f_w1, hy_f_b1, hy_f_freq1, hy_f_w2, hy_f_b2, hy_f_freq2,
              hy_f_w3, hy_f_b3, hy_f_freq3, hy_f_w4, hy_skip, hy_w_out, hy_b_out,
              ffn_w_gate, ffn_w_up, ffn_w_down, norm_final):
    h = x
    for i in range(DEPTH):
        hn = rmsnorm(h, norm_mix[i])
        j = i // N_MIXERS
        if i % N_MIXERS == 0:
            mix = conformer_conv(hn, cv_w_pw1[j], cv_b_pw1[j], cv_w_dw[j], cv_b_dw[j],
                                 cv_ln_g[j], cv_ln_b[j], cv_w_pw2[j], cv_b_pw2[j])
        else:
            mix = hyena(hn, hy_w_in[j], hy_b_in[j], hy_w_short[j], hy_b_short[j],
                        hy_f_w1[j], hy_f_b1[j], hy_f_freq1[j],
                        hy_f_w2[j], hy_f_b2[j], hy_f_freq2[j],
                        hy_f_w3[j], hy_f_b3[j], hy_f_freq3[j], hy_f_w4[j],
                        hy_skip[j], hy_w_out[j], hy_b_out[j])
        h = h + mix
        hn = rmsnorm(h, norm_ffn[i])
        h = h + swiglu(hn, ffn_w_gate[i], ffn_w_up[i], ffn_w_down[i])
    return rmsnorm(h, norm_final)
```

```python
import functools
import math

import jax
import jax.numpy as jnp
import numpy as np
from jax import lax
from jax.experimental import pallas as pl
from jax.experimental.pallas import tpu as pltpu

NORM_EPS = 1e-6
LN_EPS = 1e-5
HYENA_N_BANDS = 16
HYENA_FAST_DECAY_PCT = 0.3
HYENA_SLOW_DECAY_PCT = 1.5
HYENA_DECAY_TARGET = 1e-2

V7X_LANES = 128
V7X_SUBLANES = 8
V7X_BF16_SUBLANES = 16
V7X_VMEM_BYTES = 64 * 1024 * 1024

CONV_HALO = 16
CONV_ROW_BLOCK = 64
CONV_BLOCK = 512

F32 = jnp.float32
BF16 = jnp.bfloat16


def _vmem_limit(nbytes):
    return int(min(max(nbytes, 32 * 1024 * 1024), V7X_VMEM_BYTES - 4 * 1024 * 1024))


def _resident(block_shape, index_map):
    return pl.BlockSpec(block_shape, index_map, pipeline_mode=pl.Buffered(1))


def _rmsnorm_bf16(x, g):
    ms = jnp.mean(x * x, axis=-1, keepdims=True)
    return (x * lax.rsqrt(ms + NORM_EPS) * g).astype(BF16)


def _conf_in_kernel(x_ref, g_ref, w_ref, b_ref, u_ref, *, tn):
    d = u_ref.shape[-1]
    hn = _rmsnorm_bf16(x_ref[...], g_ref[...])
    for c in range(d // tn):
        lo, hi = c * tn, (c + 1) * tn
        a1 = jnp.dot(hn, w_ref[:, lo:hi], preferred_element_type=F32) + b_ref[:, lo:hi]
        a2 = jnp.dot(hn, w_ref[:, d + lo:d + hi], preferred_element_type=F32) + b_ref[:, d + lo:d + hi]
        u_ref[:, lo:hi] = a1 * jax.nn.sigmoid(a2)


def _conf_in(x, g, w, b, *, tm, tn):
    m, d = x.shape
    return pl.pallas_call(
        functools.partial(_conf_in_kernel, tn=tn),
        grid=(m // tm,),
        in_specs=[
            pl.BlockSpec((tm, d), lambda i: (i, 0)),
            _resident((1, d), lambda i: (0, 0)),
            _resident((d, 2 * d), lambda i: (0, 0)),
            _resident((1, 2 * d), lambda i: (0, 0)),
        ],
        out_specs=pl.BlockSpec((tm, d), lambda i: (i, 0)),
        out_shape=jax.ShapeDtypeStruct((m, d), F32),
        compiler_params=pltpu.CompilerParams(
            dimension_semantics=("arbitrary",),
            vmem_limit_bytes=_vmem_limit(2 * d * d * 2 + 4 * tm * d * 4 + 6 * tm * tn * 4 + tm * d * 2 + (4 << 20))),
        name="conf_in",
    )(x, g, w, b)


def _conf_out_kernel(u_ref, up_ref, un_ref, x_ref, wdw_ref, bdw_ref, lg_ref, lb_ref, w2_ref, b2_ref,
                     o_ref, ext_ref, a_ref, *, tiles_per_seq, width):
    tm, d = u_ref.shape
    halo = up_ref.shape[0]
    rb_rows = CONV_ROW_BLOCK
    pos = pl.program_id(0) % tiles_per_seq
    ext_ref[0:halo, :] = jnp.where(pos == 0, 0.0, up_ref[...])
    ext_ref[halo:halo + tm, :] = u_ref[...]
    ext_ref[halo + tm:, :] = jnp.where(pos == tiles_per_seq - 1, 0.0, un_ref[...])

    pad = (width - 1) // 2
    base = halo - pad
    n_lc = d // V7X_LANES

    def conv_body(idx, carry):
        r0 = pl.multiple_of((idx // n_lc) * rb_rows, rb_rows)
        l0 = pl.multiple_of((idx % n_lc) * V7X_LANES, V7X_LANES)
        win = ext_ref[pl.ds(r0, rb_rows + 2 * halo), pl.ds(l0, V7X_LANES)]
        taps = wdw_ref[:, pl.ds(l0, V7X_LANES)]
        acc = jnp.zeros((rb_rows, V7X_LANES), F32)
        for s in range(V7X_SUBLANES):
            n_q = (width - 1 - s) // V7X_SUBLANES + 1
            span = rb_rows + V7X_SUBLANES * (n_q - 1)
            sh = win[base + s:base + s + span, :]
            for q in range(n_q):
                j = V7X_SUBLANES * q + s
                acc = acc + taps[j:j + 1, :] * sh[V7X_SUBLANES * q:V7X_SUBLANES * q + rb_rows, :]
        a_ref[pl.ds(r0, rb_rows), pl.ds(l0, V7X_LANES)] = acc + bdw_ref[:, pl.ds(l0, V7X_LANES)]
        return carry

    lax.fori_loop(0, (tm // rb_rows) * n_lc, conv_body, 0)

    y = a_ref[...]
    mu = jnp.mean(y, axis=-1, keepdims=True)
    yc = y - mu
    var = jnp.mean(yc * yc, axis=-1, keepdims=True)
    z = yc * lax.rsqrt(var + LN_EPS) * lg_ref[...] + lb_ref[...]
    act = (z * jax.nn.sigmoid(z)).astype(BF16)
    o_ref[...] = x_ref[...] + jnp.dot(act, w2_ref[...], preferred_element_type=F32) + b2_ref[...]


def _conf_out(u, x, wdw, bdw, lg, lb, w2, b2, *, tm, seq, width):
    m, d = u.shape
    halo = CONV_HALO
    hb = tm // halo
    n_halo_blocks = m // halo
    kern = functools.partial(_conf_out_kernel, tiles_per_seq=seq // tm, width=width)
    return pl.pallas_call(
        kern,
        grid=(m // tm,),
        in_specs=[
            pl.BlockSpec((tm, d), lambda i: (i, 0)),
            pl.BlockSpec((halo, d), lambda i: (jnp.maximum(i * hb - 1, 0), 0)),
            pl.BlockSpec((halo, d), lambda i: (jnp.minimum((i + 1) * hb, n_halo_blocks - 1), 0)),
            pl.BlockSpec((tm, d), lambda i: (i, 0)),
            _resident(wdw.shape, lambda i: (0, 0)),
            _resident((1, d), lambda i: (0, 0)),
            _resident((1, d), lambda i: (0, 0)),
            _resident((1, d), lambda i: (0, 0)),
            _resident((d, d), lambda i: (0, 0)),
            _resident((1, d), lambda i: (0, 0)),
        ],
        out_specs=pl.BlockSpec((tm, d), lambda i: (i, 0)),
        out_shape=jax.ShapeDtypeStruct((m, d), F32),
        scratch_shapes=[pltpu.VMEM((tm + 2 * halo, d), F32), pltpu.VMEM((tm, d), F32)],
        compiler_params=pltpu.CompilerParams(
            dimension_semantics=("arbitrary",),
            vmem_limit_bytes=_vmem_limit(d * d * 2 + 12 * tm * d * 4 + (4 << 20))),
        name="conf_out",
    )(u, u, u, x, wdw, bdw, lg, lb, w2, b2)


def _ffn_kernel(x_ref, g_ref, wg_ref, wu_ref, wd_ref, gf_ref, o_ref, xn_ref, *, final_norm):
    j = pl.program_id(1)

    @pl.when(j == 0)
    def _():
        x = x_ref[...]
        xn_ref[...] = _rmsnorm_bf16(x, g_ref[...])
        o_ref[...] = x

    xn = xn_ref[...]
    gate = jnp.dot(xn, wg_ref[...], preferred_element_type=F32)
    up = jnp.dot(xn, wu_ref[...], preferred_element_type=F32)
    act = (gate * jax.nn.sigmoid(gate) * up).astype(BF16)
    o_ref[...] += jnp.dot(act, wd_ref[...], preferred_element_type=F32)

    if final_norm:
        @pl.when(j == pl.num_programs(1) - 1)
        def _():
            h = o_ref[...]
            ms = jnp.mean(h * h, axis=-1, keepdims=True)
            o_ref[...] = h * lax.rsqrt(ms + NORM_EPS) * gf_ref[...]


def _ffn(x, g, wg, wu, wd, gf, *, tm, tf, final_norm):
    m, d = x.shape
    f = wg.shape[1]
    return pl.pallas_call(
        functools.partial(_ffn_kernel, final_norm=final_norm),
        grid=(m // tm, f // tf),
        in_specs=[
            pl.BlockSpec((tm, d), lambda i, j: (i, 0)),
            _resident((1, d), lambda i, j: (0, 0)),
            pl.BlockSpec((d, tf), lambda i, j: (0, j)),
            pl.BlockSpec((d, tf), lambda i, j: (0, j)),
            pl.BlockSpec((tf, d), lambda i, j: (j, 0)),
            _resident((1, d), lambda i, j: (0, 0)),
        ],
        out_specs=pl.BlockSpec((tm, d), lambda i, j: (i, 0)),
        out_shape=jax.ShapeDtypeStruct((m, d), F32),
        scratch_shapes=[pltpu.VMEM((tm, d), BF16)],
        compiler_params=pltpu.CompilerParams(
            dimension_semantics=("arbitrary", "arbitrary"),
            vmem_limit_bytes=_vmem_limit(5 * tm * d * 4 + tm * d * 2 + 12 * d * tf + 5 * tm * tf * 4 + (4 << 20))),
        name="ffn_final" if final_norm else "ffn",
    )(x, g, wg, wu, wd, gf)


def _hy_in_kernel(x_ref, g_ref, w0_ref, w1_ref, w2_ref, bi0_ref, bi1_ref, bi2_ref,
                  ws0_ref, ws1_ref, ws2_ref, bs0_ref, bs1_ref, bs2_ref,
                  p_ref, x0_ref, xn_ref):
    j = pl.program_id(1)
    seq = x_ref.shape[1]

    @pl.when(j == 0)
    def _():
        xn_ref[...] = _rmsnorm_bf16(x_ref[0], g_ref[...])

    xn = xn_ref[...]
    row = lax.broadcasted_iota(jnp.int32, (seq, 1), 0)
    first = row == 0
    last = row == seq - 1

    def branch(w_ref, bi_ref, ws_ref, bs_ref):
        z = jnp.dot(xn, w_ref[...], preferred_element_type=F32) + bi_ref[...]
        zp = jnp.where(first, 0.0, pltpu.roll(z, 1, axis=0))
        zn = jnp.where(last, 0.0, pltpu.roll(z, seq - 1, axis=0))
        return ws_ref[0:1, :] * zp + ws_ref[1:2, :] * z + ws_ref[2:3, :] * zn + bs_ref[...]

    x0_ref[0] = branch(w0_ref, bi0_ref, ws0_ref, bs0_ref)
    x1 = branch(w1_ref, bi1_ref, ws1_ref, bs1_ref)
    v = branch(w2_ref, bi2_ref, ws2_ref, bs2_ref)
    p_ref[0] = v * x1


def _hy_in(x, g, w_in, b_in, w_short, b_short, *, tn):
    b, seq, d = x.shape
    nt = d // tn

    def col(k):
        return lambda i, j: (0, k * nt + j)

    w_specs = [pl.BlockSpec((d, tn), col(k)) for k in range(3)]
    bi_specs = [pl.BlockSpec((1, tn), col(k)) for k in range(3)]
    ws_specs = [pl.BlockSpec((w_short.shape[0], tn), col(k)) for k in range(3)]
    bs_specs = [pl.BlockSpec((1, tn), col(k)) for k in range(3)]
    out_spec = pl.BlockSpec((1, seq, tn), lambda i, j: (i, 0, j))
    return pl.pallas_call(
        _hy_in_kernel,
        grid=(b, nt),
        in_specs=[_resident((1, seq, d), lambda i, j: (i, 0, 0)), _resident((1, d), lambda i, j: (0, 0))]
        + w_specs + bi_specs + ws_specs + bs_specs,
        out_specs=[out_spec, out_spec],
        out_shape=[jax.ShapeDtypeStruct((b, seq, d), F32), jax.ShapeDtypeStruct((b, seq, d), F32)],
        scratch_shapes=[pltpu.VMEM((seq, d), BF16)],
        compiler_params=pltpu.CompilerParams(
            dimension_semantics=("arbitrary", "arbitrary"),
            vmem_limit_bytes=_vmem_limit(seq * d * 6 + 12 * d * tn + 14 * seq * tn * 4 + (4 << 20))),
        name="hy_in",
    )(x, g, w_in, w_in, w_in, b_in, b_in, b_in, w_short, w_short, w_short, b_short, b_short, b_short)


def _filter_kernel(z_ref, w1_ref, b1_ref, f1_ref, w2_ref, b2_ref, f2_ref, w3_ref, b3_ref, f3_ref,
                   w4f_ref, w4b_ref, df_ref, fwd_ref, eneg_ref, g_ref, hf_ref, *, nb, blk):
    seq = z_ref.shape[0]

    @pl.when(pl.program_id(0) == 0)
    def _():
        hf = jnp.sin(f1_ref[...] * (jnp.dot(z_ref[...], w1_ref[...], preferred_element_type=F32) + b1_ref[...]))
        hf = jnp.sin(f2_ref[...] * (jnp.dot(hf, w2_ref[...], preferred_element_type=F32) + b2_ref[...]))
        hf_ref[...] = jnp.sin(f3_ref[...] * (jnp.dot(hf, w3_ref[...], preferred_element_type=F32) + b3_ref[...]))

    hf = hf_ref[...]
    row = lax.broadcasted_iota(jnp.int32, (seq, 1), 0)
    t = row.astype(F32) * (1.0 / (seq - 1))
    decay = jnp.exp(-t * df_ref[...])
    kf = jnp.dot(hf, w4f_ref[...], preferred_element_type=F32) * decay
    kb = jnp.dot(hf, w4b_ref[...], preferred_element_type=F32) * decay
    kb = jnp.where(row == 0, 0.0, kb)

    fwd = fwd_ref[...]
    eneg = eneg_ref[...]

    def spectra(k):
        pos, neg = [], []
        for jb in range(nb):
            kj = k[jb * blk:(jb + 1) * blk, :].astype(BF16)
            pos.append(jnp.dot(fwd, kj, preferred_element_type=F32))
            neg.append(jnp.dot(eneg, kj, preferred_element_type=F32) if jb < nb - 1 else None)
        return pos, neg

    fpos, fneg = spectra(kf)
    bpos, bneg = spectra(kb)

    def store(idx, val, conj_add=None):
        re, im = val[:blk], val[blk:]
        if conj_add is not None:
            re, im = re + conj_add[:blk], im - conj_add[blk:]
        g_ref[idx, 0:blk, :] = re
        g_ref[idx, blk:, :] = im

    store(nb - 1, fpos[0], conj_add=bpos[0])
    for c in range(1, nb):
        store(nb - 1 + c, fneg[c - 1] + fpos[c])
        gb = bneg[c - 1] + bpos[c]
        g_ref[nb - 1 - c, 0:blk, :] = gb[:blk]
        g_ref[nb - 1 - c, blk:, :] = -gb[blk:]


def _filter_spectra(z, w1, b1, f1, w2, b2, f2, w3, b3, f3, w4, deltas, fwd, eneg, *, td, nb, blk):
    seq = z.shape[0]
    d = deltas.shape[1]
    nt = d // td
    order = w2.shape[0]
    full = lambda a: _resident(a.shape, lambda j: (0,) * a.ndim)
    return pl.pallas_call(
        functools.partial(_filter_kernel, nb=nb, blk=blk),
        grid=(nt,),
        in_specs=[full(z), full(w1), full(b1), full(f1), full(w2), full(b2), full(f2), full(w3), full(b3), full(f3),
                  pl.BlockSpec((order, td), lambda j: (0, j)),
                  pl.BlockSpec((order, td), lambda j: (0, nt + j)),
                  pl.BlockSpec((1, td), lambda j: (0, j)),
                  full(fwd), full(eneg)],
        out_specs=pl.BlockSpec((2 * nb - 1, 2 * blk, td), lambda j: (0, 0, j)),
        out_shape=jax.ShapeDtypeStruct((2 * nb - 1, 2 * blk, d), F32),
        scratch_shapes=[pltpu.VMEM((seq, order), F32)],
        compiler_params=pltpu.CompilerParams(
            dimension_semantics=("arbitrary",),
            vmem_limit_bytes=_vmem_limit(2 * (2 * nb - 1) * 2 * blk * td * 4 + (4 * nb) * 2 * blk * td * 4
                                         + 6 * seq * td * 4 + (8 << 20))),
        name="hy_filter",
    )(z, w1, b1, f1, w2, b2, f2, w3, b3, f3, w4, w4, deltas, fwd, eneg)


def _hy_conv_kernel(p_ref, x0_ref, g_ref, skip_ref, fwd_ref, inv_ref, o_ref, ph_ref, yh_ref, *, nb, blk):
    td = p_ref.shape[-1]
    slab = V7X_BF16_SUBLANES
    lane_w = 2 * V7X_LANES

    for b in range(nb):
        pb = p_ref[0, b * blk:(b + 1) * blk, :].astype(BF16)
        ph_ref[b] = jnp.dot(fwd_ref[...], pb, preferred_element_type=F32)

    for a in range(nb):
        def body(s, carry, a=a):
            r0 = pl.multiple_of(s * slab, slab)
            for lc in range(td // lane_w):
                ls = slice(lc * lane_w, (lc + 1) * lane_w)
                acc_re = jnp.zeros((slab, lane_w), F32)
                acc_im = jnp.zeros((slab, lane_w), F32)
                for b in range(nb):
                    c = a - b + nb - 1
                    gre = g_ref[c, pl.ds(r0, slab), ls]
                    gim = g_ref[c, pl.ds(blk + r0, slab), ls]
                    pre = ph_ref[b, pl.ds(r0, slab), ls]
                    pim = ph_ref[b, pl.ds(blk + r0, slab), ls]
                    acc_re = acc_re + (gre * pre - gim * pim)
                    acc_im = acc_im + (gre * pim + gim * pre)
                yh_ref[pl.ds(r0, slab), ls] = acc_re.astype(BF16)
                yh_ref[pl.ds(blk + r0, slab), ls] = acc_im.astype(BF16)
            return carry

        lax.fori_loop(0, blk // slab, body, 0)
        y = jnp.dot(inv_ref[...], yh_ref[...], preferred_element_type=F32)
        rows = slice(a * blk, (a + 1) * blk)
        y = y + p_ref[0, rows, :] * skip_ref[...]
        o_ref[0, rows, :] = (y * x0_ref[0, rows, :]).astype(o_ref.dtype)


def _hy_conv(p, x0, gspec, skip, fwd, inv, *, td, nb, blk):
    b, seq, d = p.shape
    io_spec = pl.BlockSpec((1, seq, td), lambda j, i: (i, 0, j))
    return pl.pallas_call(
        functools.partial(_hy_conv_kernel, nb=nb, blk=blk),
        grid=(d // td, b),
        in_specs=[io_spec, io_spec,
                  _resident((2 * nb - 1, 2 * blk, td), lambda j, i: (0, 0, j)),
                  pl.BlockSpec((1, td), lambda j, i: (0, j)),
                  _resident(fwd.shape, lambda j, i: (0, 0)),
                  _resident(inv.shape, lambda j, i: (0, 0))],
        out_specs=io_spec,
        out_shape=jax.ShapeDtypeStruct((b, seq, d), BF16),
        scratch_shapes=[pltpu.VMEM((nb, 2 * blk, td), F32), pltpu.VMEM((2 * blk, td), BF16)],
        compiler_params=pltpu.CompilerParams(
            dimension_semantics=("arbitrary", "arbitrary"),
            vmem_limit_bytes=_vmem_limit((2 * nb - 1) * 2 * blk * td * 4 + nb * 2 * blk * td * 4
                                         + 5 * seq * td * 4 + 8 * blk * td * 4 + (6 << 20))),
        name="hy_conv",
    )(p, x0, gspec, skip, fwd, inv)


def _proj_res_kernel(a_ref, x_ref, w_ref, b_ref, o_ref):
    o_ref[...] = x_ref[...] + jnp.dot(a_ref[...], w_ref[...], preferred_element_type=F32) + b_ref[...]


def _proj_res(a, x, w, b, *, tm):
    m, d = x.shape
    return pl.pallas_call(
        _proj_res_kernel,
        grid=(m // tm,),
        in_specs=[pl.BlockSpec((tm, d), lambda i: (i, 0)),
                  pl.BlockSpec((tm, d), lambda i: (i, 0)),
                  _resident((d, d), lambda i: (0, 0)),
                  _resident((1, d), lambda i: (0, 0))],
        out_specs=pl.BlockSpec((tm, d), lambda i: (i, 0)),
        out_shape=jax.ShapeDtypeStruct((m, d), F32),
        compiler_params=pltpu.CompilerParams(
            dimension_semantics=("arbitrary",),
            vmem_limit_bytes=_vmem_limit(d * d * 2 + 7 * tm * d * 4 + (4 << 20))),
        name="hy_out",
    )(a, x, w, b)


def _dft_tables(blk):
    f = (np.arange(blk, dtype=np.float64) + 0.5)[:, None]
    q = np.arange(blk, dtype=np.float64)[None, :]
    w = 2.0 * np.pi / (2 * blk)
    ang = w * f * q
    fwd = np.concatenate([np.cos(ang), -np.sin(ang)], axis=0)
    ang_n = w * f * (q - blk)
    eneg = np.concatenate([np.cos(ang_n), -np.sin(ang_n)], axis=0)
    eneg[:, 0] = 0.0
    inv = np.concatenate([np.cos(ang).T, -np.sin(ang).T], axis=1) / blk
    return (jnp.asarray(fwd, dtype=BF16), jnp.asarray(eneg, dtype=BF16), jnp.asarray(inv, dtype=BF16))


def _hyena_positional_features(seq, pad_to):
    t = jnp.linspace(0.0, 1.0, seq, dtype=F32)[:, None]
    bands = jnp.linspace(1e-4, HYENA_N_BANDS - 1, HYENA_N_BANDS, dtype=F32)[None, :]
    wpos = (2.0 * math.pi) * jnp.arange(seq, dtype=F32)[:, None] / seq
    z = jnp.concatenate([t, jnp.cos(bands * wpos), -jnp.sin(bands * wpos)], axis=-1)
    return jnp.pad(z, ((0, 0), (0, pad_to - z.shape[1])))


def _hyena_decay_rates(d):
    max_decay = math.log(HYENA_DECAY_TARGET) / HYENA_FAST_DECAY_PCT
    min_decay = math.log(HYENA_DECAY_TARGET) / HYENA_SLOW_DECAY_PCT
    return jnp.abs(jnp.linspace(min_decay, max_decay, d, dtype=F32))[None, :]


def _pick(n, prefs):
    for p in prefs:
        if n % p == 0:
            return p
    return n


def kernel(x, norm_mix, norm_ffn, cv_w_pw1, cv_b_pw1, cv_w_dw, cv_b_dw, cv_ln_g, cv_ln_b, cv_w_pw2, cv_b_pw2, hy_w_in, hy_b_in, hy_w_short, hy_b_short, hy_f_w1, hy_f_b1, hy_f_freq1, hy_f_w2, hy_f_b2, hy_f_freq2, hy_f_w3, hy_f_b3, hy_f_freq3, hy_f_w4, hy_skip, hy_w_out, hy_b_out, ffn_w_gate, ffn_w_up, ffn_w_down, norm_final):
    bsz, seq, d = x.shape
    m = bsz * seq
    f = ffn_w_gate.shape[-1]
    depth = norm_mix.shape[0]
    assert depth == 2 and cv_w_pw1.shape[0] == 1 and hy_w_in.shape[0] == 1

    tm = _pick(seq, (512, 256, 128))
    tm_ffn = _pick(m, (512, 256, 128))
    tf = _pick(f, (512, 256, 128))
    tn = _pick(d, (512, 256, 128))
    tn_hy = _pick(d, (256, 128))
    td = _pick(d, (512, 256, 128))
    blk = _pick(seq, (CONV_BLOCK, 256, 128))
    nb = seq // blk

    row = lambda a: a.reshape(1, -1).astype(F32)
    bf = lambda a: a.astype(BF16)

    h = x.reshape(m, d)

    width = cv_w_dw.shape[1]
    wdw = jnp.pad(cv_w_dw[0], ((0, 2 * CONV_HALO - width), (0, 0)))
    u = _conf_in(h, row(norm_mix[0]), bf(cv_w_pw1[0]), row(cv_b_pw1[0]), tm=tm, tn=tn)
    h = _conf_out(u, h, wdw, row(cv_b_dw[0]), row(cv_ln_g[0]), row(cv_ln_b[0]), bf(cv_w_pw2[0]),
                  row(cv_b_pw2[0]), tm=tm, seq=seq, width=width)
    h = _ffn(h, row(norm_ffn[0]), bf(ffn_w_gate[0]), bf(ffn_w_up[0]), bf(ffn_w_down[0]), row(norm_final),
             tm=tm_ffn, tf=tf, final_norm=False)

    fwd, eneg, inv = _dft_tables(blk)
    emb = hy_f_w1.shape[1]
    emb_pad = -(-emb // V7X_LANES) * V7X_LANES
    z = _hyena_positional_features(seq, emb_pad)
    w1 = jnp.pad(hy_f_w1[0], ((0, emb_pad - emb), (0, 0)))
    gspec = _filter_spectra(z, w1, row(hy_f_b1[0]), row(hy_f_freq1[0]), hy_f_w2[0], row(hy_f_b2[0]),
                            row(hy_f_freq2[0]), hy_f_w3[0], row(hy_f_b3[0]), row(hy_f_freq3[0]), hy_f_w4[0],
                            _hyena_decay_rates(d), fwd, eneg, td=td, nb=nb, blk=blk)
    p, x0 = _hy_in(h.reshape(bsz, seq, d), row(norm_mix[1]), bf(hy_w_in[0]), row(hy_b_in[0]), hy_w_short[0],
                   row(hy_b_short[0]), tn=tn_hy)
    gated = _hy_conv(p, x0, gspec, row(hy_skip[0]), fwd, inv, td=td, nb=nb, blk=blk)
    h = _proj_res(gated.reshape(m, d), h, bf(hy_w_out[0]), row(hy_b_out[0]), tm=tm)
    h = _ffn(h, row(norm_ffn[1]), bf(ffn_w_gate[1]), bf(ffn_w_up[1]), bf(ffn_w_down[1]), row(norm_final),
             tm=tm_ffn, tf=tf, final_norm=True)
    return h.reshape(bsz, seq, d)
```

```python
import functools
import math

import jax
import jax.numpy as jnp
import numpy as np
from jax import lax
from jax.experimental import pallas as pl
from jax.experimental.pallas import tpu as pltpu

NORM_EPS = 1e-6
LN_EPS = 1e-5
HYENA_N_BANDS = 16
HYENA_FAST_DECAY_PCT = 0.3
HYENA_SLOW_DECAY_PCT = 1.5
HYENA_DECAY_TARGET = 1e-2

V7X_LANES = 128
V7X_SUBLANES = 8
V7X_BF16_SUBLANES = 16
V7X_VMEM_BYTES = 64 * 1024 * 1024

DW_BLOCK = 256
CONV_BLOCK = 512

F32 = jnp.float32
BF16 = jnp.bfloat16


def _vmem_limit(nbytes):
    return int(min(max(nbytes, 32 * 1024 * 1024), V7X_VMEM_BYTES - 4 * 1024 * 1024))


def _resident(block_shape, index_map):
    return pl.BlockSpec(block_shape, index_map, pipeline_mode=pl.Buffered(1))


def _rmsnorm_bf16(x, g):
    ms = jnp.mean(x * x, axis=-1, keepdims=True)
    return (x * lax.rsqrt(ms + NORM_EPS) * g).astype(BF16)


def _spectral_mix(g_ref, ph_ref, yh_ref, a, *, nb, blk, lag_blocks):
    lanes = yh_ref.shape[-1]
    slab = V7X_BF16_SUBLANES
    lane_w = min(lanes, 2 * V7X_LANES)
    for s in range(blk // slab):
        r_re = slice(s * slab, (s + 1) * slab)
        r_im = slice(blk + s * slab, blk + (s + 1) * slab)
        for lc in range(lanes // lane_w):
            ls = slice(lc * lane_w, (lc + 1) * lane_w)
            acc_re = acc_im = None
            for b in range(max(0, a - lag_blocks), min(nb, a + lag_blocks + 1)):
                gi = a - b + lag_blocks
                gre, gim = g_ref[gi, r_re, ls], g_ref[gi, r_im, ls]
                pre, pim = ph_ref[b, r_re, ls], ph_ref[b, r_im, ls]
                t_re = gre * pre - gim * pim
                t_im = gre * pim + gim * pre
                acc_re = t_re if acc_re is None else acc_re + t_re
                acc_im = t_im if acc_im is None else acc_im + t_im
            yh_ref[r_re, ls] = acc_re.astype(yh_ref.dtype)
            yh_ref[r_im, ls] = acc_im.astype(yh_ref.dtype)


def _dw_spectra_kernel(e_ref, w_ref, o_ref):
    for c in range(e_ref.shape[0]):
        o_ref[c] = jnp.dot(e_ref[c], w_ref[...], preferred_element_type=F32)


def _dw_spectra(etab, wdw, *, td):
    nc, rows, taps = etab.shape
    d = wdw.shape[1]
    return pl.pallas_call(
        _dw_spectra_kernel,
        grid=(d // td,),
        in_specs=[_resident(etab.shape, lambda j: (0, 0, 0)), pl.BlockSpec((taps, td), lambda j: (0, j))],
        out_specs=pl.BlockSpec((nc, rows, td), lambda j: (0, 0, j)),
        out_shape=jax.ShapeDtypeStruct((nc, rows, d), F32),
        compiler_params=pltpu.CompilerParams(dimension_semantics=("arbitrary",)),
        name="dw_spectra",
    )(etab, wdw)


def _conf_in_kernel(x_ref, g_ref, wa_ref, wb_ref, ba_ref, bb_ref, gs_ref, bdw_ref, fwd_ref, inv_ref,
                    y_ref, xn_ref, ph_ref, yh_ref, *, nb, blk):
    @pl.when(pl.program_id(1) == 0)
    def _():
        xn_ref[...] = _rmsnorm_bf16(x_ref[0], g_ref[...])

    xn = xn_ref[...]
    a1 = jnp.dot(xn, wa_ref[...], preferred_element_type=F32) + ba_ref[...]
    a2 = jnp.dot(xn, wb_ref[...], preferred_element_type=F32) + bb_ref[...]
    u = (a1 * jax.nn.sigmoid(a2)).astype(BF16)
    for b in range(nb):
        ph_ref[b] = jnp.dot(fwd_ref[...], u[b * blk:(b + 1) * blk, :], preferred_element_type=F32)
    for a in range(nb):
        _spectral_mix(gs_ref, ph_ref, yh_ref, a, nb=nb, blk=blk, lag_blocks=1)
        y = jnp.dot(inv_ref[...], yh_ref[...], preferred_element_type=F32)
        y_ref[0, a * blk:(a + 1) * blk, :] = y + bdw_ref[...]


def _conf_in(x, g, w, b, gs, bdw, fwd, inv, *, tn, blk):
    bsz, seq, d = x.shape
    nt = d // tn
    nb = seq // blk
    return pl.pallas_call(
        functools.partial(_conf_in_kernel, nb=nb, blk=blk),
        grid=(bsz, nt),
        in_specs=[
            _resident((1, seq, d), lambda i, j: (i, 0, 0)),
            _resident((1, d), lambda i, j: (0, 0)),
            pl.BlockSpec((d, tn), lambda i, j: (0, j)),
            pl.BlockSpec((d, tn), lambda i, j: (0, nt + j)),
            pl.BlockSpec((1, tn), lambda i, j: (0, j)),
            pl.BlockSpec((1, tn), lambda i, j: (0, nt + j)),
            pl.BlockSpec((gs.shape[0], 2 * blk, tn), lambda i, j: (0, 0, j)),
            pl.BlockSpec((1, tn), lambda i, j: (0, j)),
            _resident(fwd.shape, lambda i, j: (0, 0)),
            _resident(inv.shape, lambda i, j: (0, 0)),
        ],
        out_specs=pl.BlockSpec((1, seq, tn), lambda i, j: (i, 0, j)),
        out_shape=jax.ShapeDtypeStruct((bsz, seq, d), F32),
        scratch_shapes=[pltpu.VMEM((seq, d), BF16), pltpu.VMEM((nb, 2 * blk, tn), F32),
                        pltpu.VMEM((2 * blk, tn), BF16)],
        compiler_params=pltpu.CompilerParams(
            dimension_semantics=("arbitrary", "arbitrary"),
            vmem_limit_bytes=_vmem_limit(seq * d * 6 + 8 * d * tn + 14 * seq * tn * 4 + (6 << 20))),
        name="conf_in",
    )(x, g, w, w, b, b, gs, bdw, fwd, inv)


def _conf_out_kernel(y_ref, x_ref, lg_ref, lb_ref, w2_ref, b2_ref, o_ref):
    y = y_ref[...]
    mu = jnp.mean(y, axis=-1, keepdims=True)
    yc = y - mu
    var = jnp.mean(yc * yc, axis=-1, keepdims=True)
    z = yc * lax.rsqrt(var + LN_EPS) * lg_ref[...] + lb_ref[...]
    act = (z * jax.nn.sigmoid(z)).astype(BF16)
    o_ref[...] = x_ref[...] + jnp.dot(act, w2_ref[...], preferred_element_type=F32) + b2_ref[...]


def _conf_out(y, x, lg, lb, w2, b2, *, tm):
    m, d = y.shape
    return pl.pallas_call(
        _conf_out_kernel,
        grid=(m // tm,),
        in_specs=[
            pl.BlockSpec((tm, d), lambda i: (i, 0)),
            pl.BlockSpec((tm, d), lambda i: (i, 0)),
            _resident((1, d), lambda i: (0, 0)),
            _resident((1, d), lambda i: (0, 0)),
            _resident((d, d), lambda i: (0, 0)),
            _resident((1, d), lambda i: (0, 0)),
        ],
        out_specs=pl.BlockSpec((tm, d), lambda i: (i, 0)),
        out_shape=jax.ShapeDtypeStruct((m, d), F32),
        compiler_params=pltpu.CompilerParams(
            dimension_semantics=("arbitrary",),
            vmem_limit_bytes=_vmem_limit(d * d * 2 + 12 * tm * d * 4 + (4 << 20))),
        name="conf_out",
    )(y, x, lg, lb, w2, b2)


def _ffn_kernel(x_ref, g_ref, wg_ref, wu_ref, wd_ref, gf_ref, o_ref, xn_ref, *, final_norm, tn):
    j = pl.program_id(1)
    d = o_ref.shape[-1]

    @pl.when(j == 0)
    def _():
        x = x_ref[...]
        xn_ref[...] = _rmsnorm_bf16(x, g_ref[...])
        o_ref[...] = x

    xn = xn_ref[...]
    gate = jnp.dot(xn, wg_ref[...], preferred_element_type=F32)
    up = jnp.dot(xn, wu_ref[...], preferred_element_type=F32)
    act = (gate * jax.nn.sigmoid(gate) * up).astype(BF16)
    for c in range(d // tn):
        cs = slice(c * tn, (c + 1) * tn)
        o_ref[:, cs] += jnp.dot(act, wd_ref[:, cs], preferred_element_type=F32)

    if final_norm:
        @pl.when(j == pl.num_programs(1) - 1)
        def _():
            h = o_ref[...]
            ms = jnp.mean(h * h, axis=-1, keepdims=True)
            o_ref[...] = h * lax.rsqrt(ms + NORM_EPS) * gf_ref[...]


def _ffn(x, g, wg, wu, wd, gf, *, tm, tf, tn, final_norm):
    m, d = x.shape
    f = wg.shape[1]
    return pl.pallas_call(
        functools.partial(_ffn_kernel, final_norm=final_norm, tn=tn),
        grid=(m // tm, f // tf),
        in_specs=[
            _resident((tm, d), lambda i, j: (i, 0)),
            _resident((1, d), lambda i, j: (0, 0)),
            pl.BlockSpec((d, tf), lambda i, j: (0, j)),
            pl.BlockSpec((d, tf), lambda i, j: (0, j)),
            pl.BlockSpec((tf, d), lambda i, j: (j, 0)),
            _resident((1, d), lambda i, j: (0, 0)),
        ],
        out_specs=pl.BlockSpec((tm, d), lambda i, j: (i, 0)),
        out_shape=jax.ShapeDtypeStruct((m, d), F32),
        scratch_shapes=[pltpu.VMEM((tm, d), BF16)],
        compiler_params=pltpu.CompilerParams(
            dimension_semantics=("arbitrary", "arbitrary"),
            vmem_limit_bytes=_vmem_limit(3 * tm * d * 4 + tm * d * 2 + 12 * d * tf + 8 * tm * tf * 4 + (8 << 20))),
        name="ffn_final" if final_norm else "ffn",
    )(x, g, wg, wu, wd, gf)


def _hy_in_kernel(x_ref, g_ref, w0_ref, w1_ref, w2_ref, bi0_ref, bi1_ref, bi2_ref,
                  ws0_ref, ws1_ref, ws2_ref, bs0_ref, bs1_ref, bs2_ref,
                  p_ref, x0_ref, xn_ref):
    j = pl.program_id(1)
    seq = x_ref.shape[1]

    @pl.when(j == 0)
    def _():
        xn_ref[...] = _rmsnorm_bf16(x_ref[0], g_ref[...])

    xn = xn_ref[...]
    row = lax.broadcasted_iota(jnp.int32, (seq, 1), 0)
    first = row == 0
    last = row == seq - 1

    def branch(w_ref, bi_ref, ws_ref, bs_ref):
        z = jnp.dot(xn, w_ref[...], preferred_element_type=F32) + bi_ref[...]
        zp = jnp.where(first, 0.0, pltpu.roll(z, 1, axis=0))
        zn = jnp.where(last, 0.0, pltpu.roll(z, seq - 1, axis=0))
        return ws_ref[0:1, :] * zp + ws_ref[1:2, :] * z + ws_ref[2:3, :] * zn + bs_ref[...]

    x0_ref[0] = branch(w0_ref, bi0_ref, ws0_ref, bs0_ref)
    x1 = branch(w1_ref, bi1_ref, ws1_ref, bs1_ref)
    v = branch(w2_ref, bi2_ref, ws2_ref, bs2_ref)
    p_ref[0] = v * x1


def _hy_in(x, g, w_in, b_in, w_short, b_short, *, tn):
    b, seq, d = x.shape
    nt = d // tn

    def col(k):
        return lambda i, j: (0, k * nt + j)

    w_specs = [pl.BlockSpec((d, tn), col(k)) for k in range(3)]
    bi_specs = [pl.BlockSpec((1, tn), col(k)) for k in range(3)]
    ws_specs = [pl.BlockSpec((w_short.shape[0], tn), col(k)) for k in range(3)]
    bs_specs = [pl.BlockSpec((1, tn), col(k)) for k in range(3)]
    out_spec = pl.BlockSpec((1, seq, tn), lambda i, j: (i, 0, j))
    return pl.pallas_call(
        _hy_in_kernel,
        grid=(b, nt),
        in_specs=[_resident((1, seq, d), lambda i, j: (i, 0, 0)), _resident((1, d), lambda i, j: (0, 0))]
        + w_specs + bi_specs + ws_specs + bs_specs,
        out_specs=[out_spec, out_spec],
        out_shape=[jax.ShapeDtypeStruct((b, seq, d), F32), jax.ShapeDtypeStruct((b, seq, d), F32)],
        scratch_shapes=[pltpu.VMEM((seq, d), BF16)],
        compiler_params=pltpu.CompilerParams(
            dimension_semantics=("arbitrary", "arbitrary"),
            vmem_limit_bytes=_vmem_limit(seq * d * 6 + 12 * d * tn + 14 * seq * tn * 4 + (4 << 20))),
        name="hy_in",
    )(x, g, w_in, w_in, w_in, b_in, b_in, b_in, w_short, w_short, w_short, b_short, b_short, b_short)


def _filter_kernel(z_ref, w1_ref, b1_ref, f1_ref, w2_ref, b2_ref, f2_ref, w3_ref, b3_ref, f3_ref,
                   w4f_ref, w4b_ref, df_ref, fwd_ref, eneg_ref, g_ref, hf_ref, *, nb, blk):
    seq = z_ref.shape[0]

    @pl.when(pl.program_id(0) == 0)
    def _():
        hf = jnp.sin(f1_ref[...] * (jnp.dot(z_ref[...], w1_ref[...], preferred_element_type=F32) + b1_ref[...]))
        hf = jnp.sin(f2_ref[...] * (jnp.dot(hf, w2_ref[...], preferred_element_type=F32) + b2_ref[...]))
        hf_ref[...] = jnp.sin(f3_ref[...] * (jnp.dot(hf, w3_ref[...], preferred_element_type=F32) + b3_ref[...]))

    hf = hf_ref[...]
    row = lax.broadcasted_iota(jnp.int32, (seq, 1), 0)
    t = row.astype(F32) * (1.0 / (seq - 1))
    decay = jnp.exp(-t * df_ref[...])
    kf = jnp.dot(hf, w4f_ref[...], preferred_element_type=F32) * decay
    kb = jnp.dot(hf, w4b_ref[...], preferred_element_type=F32) * decay
    kb = jnp.where(row == 0, 0.0, kb)

    fwd = fwd_ref[...]
    eneg = eneg_ref[...]

    def spectra(k):
        pos, neg = [], []
        for jb in range(nb):
            kj = k[jb * blk:(jb + 1) * blk, :].astype(BF16)
            pos.append(jnp.dot(fwd, kj, preferred_element_type=F32))
            neg.append(jnp.dot(eneg, kj, preferred_element_type=F32) if jb < nb - 1 else None)
        return pos, neg

    fpos, fneg = spectra(kf)
    bpos, bneg = spectra(kb)

    def store(idx, val, conj_add=None):
        re, im = val[:blk], val[blk:]
        if conj_add is not None:
            re, im = re + conj_add[:blk], im - conj_add[blk:]
        g_ref[idx, 0:blk, :] = re
        g_ref[idx, blk:, :] = im

    store(nb - 1, fpos[0], conj_add=bpos[0])
    for c in range(1, nb):
        store(nb - 1 + c, fneg[c - 1] + fpos[c])
        gb = bneg[c - 1] + bpos[c]
        g_ref[nb - 1 - c, 0:blk, :] = gb[:blk]
        g_ref[nb - 1 - c, blk:, :] = -gb[blk:]


def _filter_spectra(z, w1, b1, f1, w2, b2, f2, w3, b3, f3, w4, deltas, fwd, eneg, *, td, nb, blk):
    seq = z.shape[0]
    d = deltas.shape[1]
    nt = d // td
    order = w2.shape[0]
    full = lambda a: _resident(a.shape, lambda j: (0,) * a.ndim)
    return pl.pallas_call(
        functools.partial(_filter_kernel, nb=nb, blk=blk),
        grid=(nt,),
        in_specs=[full(z), full(w1), full(b1), full(f1), full(w2), full(b2), full(f2), full(w3), full(b3), full(f3),
                  pl.BlockSpec((order, td), lambda j: (0, j)),
                  pl.BlockSpec((order, td), lambda j: (0, nt + j)),
                  pl.BlockSpec((1, td), lambda j: (0, j)),
                  full(fwd), full(eneg)],
        out_specs=pl.BlockSpec((2 * nb - 1, 2 * blk, td), lambda j: (0, 0, j)),
        out_shape=jax.ShapeDtypeStruct((2 * nb - 1, 2 * blk, d), F32),
        scratch_shapes=[pltpu.VMEM((seq, order), F32)],
        compiler_params=pltpu.CompilerParams(
            dimension_semantics=("arbitrary",),
            vmem_limit_bytes=_vmem_limit(2 * (2 * nb - 1) * 2 * blk * td * 4 + (4 * nb) * 2 * blk * td * 4
                                         + 6 * seq * td * 4 + (8 << 20))),
        name="hy_filter",
    )(z, w1, b1, f1, w2, b2, f2, w3, b3, f3, w4, w4, deltas, fwd, eneg)


def _hy_conv_kernel(p_ref, x0_ref, g_ref, skip_ref, fwd_ref, inv_ref, o_ref, ph_ref, yh_ref, *, nb, blk):
    for b in range(nb):
        pb = p_ref[0, b * blk:(b + 1) * blk, :].astype(BF16)
        ph_ref[b] = jnp.dot(fwd_ref[...], pb, preferred_element_type=F32)

    for a in range(nb):
        _spectral_mix(g_ref, ph_ref, yh_ref, a, nb=nb, blk=blk, lag_blocks=nb - 1)
        y = jnp.dot(inv_ref[...], yh_ref[...], preferred_element_type=F32)
        rows = slice(a * blk, (a + 1) * blk)
        y = y + p_ref[0, rows, :] * skip_ref[...]
        o_ref[0, rows, :] = (y * x0_ref[0, rows, :]).astype(o_ref.dtype)


def _hy_conv(p, x0, gspec, skip, fwd, inv, *, td, nb, blk):
    b, seq, d = p.shape
    io_spec = pl.BlockSpec((1, seq, td), lambda j, i: (i, 0, j))
    return pl.pallas_call(
        functools.partial(_hy_conv_kernel, nb=nb, blk=blk),
        grid=(d // td, b),
        in_specs=[io_spec, io_spec,
                  _resident((2 * nb - 1, 2 * blk, td), lambda j, i: (0, 0, j)),
                  pl.BlockSpec((1, td), lambda j, i: (0, j)),
                  _resident(fwd.shape, lambda j, i: (0, 0)),
                  _resident(inv.shape, lambda j, i: (0, 0))],
        out_specs=io_spec,
        out_shape=jax.ShapeDtypeStruct((b, seq, d), BF16),
        scratch_shapes=[pltpu.VMEM((nb, 2 * blk, td), F32), pltpu.VMEM((2 * blk, td), BF16)],
        compiler_params=pltpu.CompilerParams(
            dimension_semantics=("arbitrary", "arbitrary"),
            vmem_limit_bytes=_vmem_limit((2 * nb - 1) * 2 * blk * td * 4 + nb * 2 * blk * td * 4
                                         + 5 * seq * td * 4 + 8 * blk * td * 4 + (6 << 20))),
        name="hy_conv",
    )(p, x0, gspec, skip, fwd, inv)


def _proj_res_kernel(a_ref, x_ref, w_ref, b_ref, o_ref):
    o_ref[...] = x_ref[...] + jnp.dot(a_ref[...], w_ref[...], preferred_element_type=F32) + b_ref[...]


def _proj_res(a, x, w, b, *, tm):
    m, d = x.shape
    return pl.pallas_call(
        _proj_res_kernel,
        grid=(m // tm,),
        in_specs=[pl.BlockSpec((tm, d), lambda i: (i, 0)),
                  pl.BlockSpec((tm, d), lambda i: (i, 0)),
                  _resident((d, d), lambda i: (0, 0)),
                  _resident((1, d), lambda i: (0, 0))],
        out_specs=pl.BlockSpec((tm, d), lambda i: (i, 0)),
        out_shape=jax.ShapeDtypeStruct((m, d), F32),
        compiler_params=pltpu.CompilerParams(
            dimension_semantics=("arbitrary",),
            vmem_limit_bytes=_vmem_limit(d * d * 2 + 7 * tm * d * 4 + (4 << 20))),
        name="hy_out",
    )(a, x, w, b)


def _dft_angles(blk):
    f = (np.arange(blk, dtype=np.float64) + 0.5)[:, None]
    return f * (2.0 * np.pi / (2 * blk))


def _dft_tables(blk):
    w = _dft_angles(blk)
    q = np.arange(blk, dtype=np.float64)[None, :]
    ang = w * q
    fwd = np.concatenate([np.cos(ang), -np.sin(ang)], axis=0)
    ang_n = w * (q - blk)
    eneg = np.concatenate([np.cos(ang_n), -np.sin(ang_n)], axis=0)
    eneg[:, 0] = 0.0
    inv = np.concatenate([np.cos(ang).T, -np.sin(ang).T], axis=1) / blk
    return (jnp.asarray(fwd, dtype=BF16), jnp.asarray(eneg, dtype=BF16), jnp.asarray(inv, dtype=BF16))


def _dw_tables(blk, width, taps_pad):
    w = _dft_angles(blk)
    pad = (width - 1) // 2
    tab = np.zeros((3, 2 * blk, taps_pad), dtype=np.float64)
    for ci, c in enumerate((-1, 0, 1)):
        for j in range(width):
            e = pad - j - blk * c
            if abs(e) <= blk - 1:
                tab[ci, :blk, j] = np.cos(w[:, 0] * e)
                tab[ci, blk:, j] = -np.sin(w[:, 0] * e)
    return jnp.asarray(tab, dtype=F32)


def _hyena_positional_features(seq, pad_to):
    t = jnp.linspace(0.0, 1.0, seq, dtype=F32)[:, None]
    bands = jnp.linspace(1e-4, HYENA_N_BANDS - 1, HYENA_N_BANDS, dtype=F32)[None, :]
    wpos = (2.0 * math.pi) * jnp.arange(seq, dtype=F32)[:, None] / seq
    z = jnp.concatenate([t, jnp.cos(bands * wpos), -jnp.sin(bands * wpos)], axis=-1)
    return jnp.pad(z, ((0, 0), (0, pad_to - z.shape[1])))


def _hyena_decay_rates(d):
    max_decay = math.log(HYENA_DECAY_TARGET) / HYENA_FAST_DECAY_PCT
    min_decay = math.log(HYENA_DECAY_TARGET) / HYENA_SLOW_DECAY_PCT
    return jnp.abs(jnp.linspace(min_decay, max_decay, d, dtype=F32))[None, :]


def _pick(n, prefs):
    for p in prefs:
        if n % p == 0:
            return p
    return n


def kernel(x, norm_mix, norm_ffn, cv_w_pw1, cv_b_pw1, cv_w_dw, cv_b_dw, cv_ln_g, cv_ln_b, cv_w_pw2, cv_b_pw2, hy_w_in, hy_b_in, hy_w_short, hy_b_short, hy_f_w1, hy_f_b1, hy_f_freq1, hy_f_w2, hy_f_b2, hy_f_freq2, hy_f_w3, hy_f_b3, hy_f_freq3, hy_f_w4, hy_skip, hy_w_out, hy_b_out, ffn_w_gate, ffn_w_up, ffn_w_down, norm_final):
    bsz, seq, d = x.shape
    m = bsz * seq
    f = ffn_w_gate.shape[-1]
    depth = norm_mix.shape[0]
    assert depth == 2 and cv_w_pw1.shape[0] == 1 and hy_w_in.shape[0] == 1

    tm = _pick(m, (512, 256, 128))
    tm_ffn = _pick(m, (1024, 512, 256, 128))
    tf = _pick(f, (512, 256, 128))
    tn = _pick(d, (512, 256, 128))
    tn_col = _pick(d, (256, 128))
    td = _pick(d, (512, 256, 128))
    blk = _pick(seq, (CONV_BLOCK, 256, 128))
    nb = seq // blk
    dw_blk = _pick(seq, (DW_BLOCK, 128))
    width = cv_w_dw.shape[1]
    assert (width - 1) // 2 < dw_blk

    row = lambda a: a.reshape(1, -1).astype(F32)
    bf = lambda a: a.astype(BF16)

    h = x.reshape(m, d)

    taps_pad = -(-width // V7X_SUBLANES) * V7X_SUBLANES
    wdw = jnp.pad(cv_w_dw[0], ((0, taps_pad - width), (0, 0)))
    dw_fwd, _, dw_inv = _dft_tables(dw_blk)
    dw_spec = _dw_spectra(_dw_tables(dw_blk, width, taps_pad), wdw, td=td)
    y = _conf_in(x, row(norm_mix[0]), bf(cv_w_pw1[0]), row(cv_b_pw1[0]), dw_spec, row(cv_b_dw[0]),
                 dw_fwd, dw_inv, tn=tn_col, blk=dw_blk)
    h = _conf_out(y.reshape(m, d), h, row(cv_ln_g[0]), row(cv_ln_b[0]), bf(cv_w_pw2[0]), row(cv_b_pw2[0]), tm=tm)
    h = _ffn(h, row(norm_ffn[0]), bf(ffn_w_gate[0]), bf(ffn_w_up[0]), bf(ffn_w_down[0]), row(norm_final),
             tm=tm_ffn, tf=tf, tn=tn, final_norm=False)

    fwd, eneg, inv = _dft_tables(blk)
    emb = hy_f_w1.shape[1]
    emb_pad = -(-emb // V7X_LANES) * V7X_LANES
    z = _hyena_positional_features(seq, emb_pad)
    w1 = jnp.pad(hy_f_w1[0], ((0, emb_pad - emb), (0, 0)))
    gspec = _filter_spectra(z, w1, row(hy_f_b1[0]), row(hy_f_freq1[0]), hy_f_w2[0], row(hy_f_b2[0]),
                            row(hy_f_freq2[0]), hy_f_w3[0], row(hy_f_b3[0]), row(hy_f_freq3[0]), hy_f_w4[0],
                            _hyena_decay_rates(d), fwd, eneg, td=td, nb=nb, blk=blk)
    p, x0 = _hy_in(h.reshape(bsz, seq, d), row(norm_mix[1]), bf(hy_w_in[0]), row(hy_b_in[0]), hy_w_short[0],
                   row(hy_b_short[0]), tn=tn_col)
    gated = _hy_conv(p, x0, gspec, row(hy_skip[0]), fwd, inv, td=td, nb=nb, blk=blk)
    h = _proj_res(gated.reshape(m, d), h, bf(hy_w_out[0]), row(hy_b_out[0]), tm=tm)
    h = _ffn(h, row(norm_ffn[1]), bf(ffn_w_gate[1]), bf(ffn_w_up[1]), bf(ffn_w_down[1]), row(norm_final),
             tm=tm_ffn, tf=tf, tn=tn, final_norm=True)
    return h.reshape(bsz, seq, d)
```

```python
import functools
import math

import jax
import jax.numpy as jnp
import numpy as np
from jax import lax
from jax.experimental import pallas as pl
from jax.experimental.pallas import tpu as pltpu

NORM_EPS = 1e-6
LN_EPS = 1e-5
HYENA_N_BANDS = 16
HYENA_FAST_DECAY_PCT = 0.3
HYENA_SLOW_DECAY_PCT = 1.5
HYENA_DECAY_TARGET = 1e-2

V7X_LANES = 128
V7X_SUBLANES = 8
V7X_BF16_SUBLANES = 16
V7X_VMEM_BYTES = 64 * 1024 * 1024

DW_BLOCK = 256
CONV_BLOCK = 512

F32 = jnp.float32
BF16 = jnp.bfloat16


def _vmem_limit(nbytes):
    return int(min(max(nbytes, 32 * 1024 * 1024), V7X_VMEM_BYTES - 4 * 1024 * 1024))


def _resident(block_shape, index_map):
    return pl.BlockSpec(block_shape, index_map, pipeline_mode=pl.Buffered(1))


def _rmsnorm_bf16(x, g):
    ms = jnp.mean(x * x, axis=-1, keepdims=True)
    return (x * lax.rsqrt(ms + NORM_EPS) * g).astype(BF16)


def _spectral_mix(g_ref, ph_ref, yh_ref, a, *, nb, blk, lag_blocks):
    lanes = yh_ref.shape[-1]
    slab = V7X_BF16_SUBLANES
    lane_w = min(lanes, 2 * V7X_LANES)
    for s in range(blk // slab):
        r_re = slice(s * slab, (s + 1) * slab)
        r_im = slice(blk + s * slab, blk + (s + 1) * slab)
        for lc in range(lanes // lane_w):
            ls = slice(lc * lane_w, (lc + 1) * lane_w)
            acc_re = acc_im = None
            for b in range(max(0, a - lag_blocks), min(nb, a + lag_blocks + 1)):
                gi = a - b + lag_blocks
                gre, gim = g_ref[gi, r_re, ls], g_ref[gi, r_im, ls]
                pre, pim = ph_ref[b, r_re, ls], ph_ref[b, r_im, ls]
                t_re = gre * pre - gim * pim
                t_im = gre * pim + gim * pre
                acc_re = t_re if acc_re is None else acc_re + t_re
                acc_im = t_im if acc_im is None else acc_im + t_im
            yh_ref[r_re, ls] = acc_re.astype(yh_ref.dtype)
            yh_ref[r_im, ls] = acc_im.astype(yh_ref.dtype)


def _dw_spectra_kernel(e_ref, w_ref, o_ref):
    for c in range(e_ref.shape[0]):
        o_ref[c] = jnp.dot(e_ref[c], w_ref[...], preferred_element_type=F32)


def _dw_spectra(etab, wdw, *, td):
    nc, rows, taps = etab.shape
    d = wdw.shape[1]
    return pl.pallas_call(
        _dw_spectra_kernel,
        grid=(d // td,),
        in_specs=[_resident(etab.shape, lambda j: (0, 0, 0)), pl.BlockSpec((taps, td), lambda j: (0, j))],
        out_specs=pl.BlockSpec((nc, rows, td), lambda j: (0, 0, j)),
        out_shape=jax.ShapeDtypeStruct((nc, rows, d), F32),
        compiler_params=pltpu.CompilerParams(dimension_semantics=("arbitrary",)),
        name="dw_spectra",
    )(etab, wdw)


def _conf_in_kernel(x_ref, g_ref, wa_ref, wb_ref, ba_ref, bb_ref, gs_ref, bdw_ref, fwd_ref, inv_ref,
                    y_ref, xn_ref, ph_ref, yh_ref, *, nb, blk):
    @pl.when(pl.program_id(1) == 0)
    def _():
        xn_ref[...] = _rmsnorm_bf16(x_ref[0], g_ref[...])

    xn = xn_ref[...]
    a1 = jnp.dot(xn, wa_ref[...], preferred_element_type=F32) + ba_ref[...]
    a2 = jnp.dot(xn, wb_ref[...], preferred_element_type=F32) + bb_ref[...]
    u = (a1 * jax.nn.sigmoid(a2)).astype(BF16)
    for b in range(nb):
        ph_ref[b] = jnp.dot(fwd_ref[...], u[b * blk:(b + 1) * blk, :], preferred_element_type=F32)
    for a in range(nb):
        _spectral_mix(gs_ref, ph_ref, yh_ref, a, nb=nb, blk=blk, lag_blocks=1)
        y = jnp.dot(inv_ref[...], yh_ref[...], preferred_element_type=F32)
        y_ref[0, a * blk:(a + 1) * blk, :] = y + bdw_ref[...]


def _conf_in(x, g, w, b, gs, bdw, fwd, inv, *, tn, blk):
    bsz, seq, d = x.shape
    nt = d // tn
    nb = seq // blk
    return pl.pallas_call(
        functools.partial(_conf_in_kernel, nb=nb, blk=blk),
        grid=(bsz, nt),
        in_specs=[
            _resident((1, seq, d), lambda i, j: (i, 0, 0)),
            _resident((1, d), lambda i, j: (0, 0)),
            pl.BlockSpec((d, tn), lambda i, j: (0, j)),
            pl.BlockSpec((d, tn), lambda i, j: (0, nt + j)),
            pl.BlockSpec((1, tn), lambda i, j: (0, j)),
            pl.BlockSpec((1, tn), lambda i, j: (0, nt + j)),
            pl.BlockSpec((gs.shape[0], 2 * blk, tn), lambda i, j: (0, 0, j)),
            pl.BlockSpec((1, tn), lambda i, j: (0, j)),
            _resident(fwd.shape, lambda i, j: (0, 0)),
            _resident(inv.shape, lambda i, j: (0, 0)),
        ],
        out_specs=pl.BlockSpec((1, seq, tn), lambda i, j: (i, 0, j)),
        out_shape=jax.ShapeDtypeStruct((bsz, seq, d), F32),
        scratch_shapes=[pltpu.VMEM((seq, d), BF16), pltpu.VMEM((nb, 2 * blk, tn), F32),
                        pltpu.VMEM((2 * blk, tn), BF16)],
        compiler_params=pltpu.CompilerParams(
            dimension_semantics=("arbitrary", "arbitrary"),
            vmem_limit_bytes=_vmem_limit(seq * d * 6 + 8 * d * tn + 14 * seq * tn * 4 + (6 << 20))),
        name="conf_in",
    )(x, g, w, w, b, b, gs, bdw, fwd, inv)


def _conf_out_kernel(y_ref, x_ref, lg_ref, lb_ref, w2_ref, b2_ref, o_ref):
    y = y_ref[...]
    mu = jnp.mean(y, axis=-1, keepdims=True)
    yc = y - mu
    var = jnp.mean(yc * yc, axis=-1, keepdims=True)
    z = yc * lax.rsqrt(var + LN_EPS) * lg_ref[...] + lb_ref[...]
    act = (z * jax.nn.sigmoid(z)).astype(BF16)
    o_ref[...] = x_ref[...] + jnp.dot(act, w2_ref[...], preferred_element_type=F32) + b2_ref[...]


def _conf_out(y, x, lg, lb, w2, b2, *, tm):
    m, d = y.shape
    return pl.pallas_call(
        _conf_out_kernel,
        grid=(m // tm,),
        in_specs=[
            pl.BlockSpec((tm, d), lambda i: (i, 0)),
            pl.BlockSpec((tm, d), lambda i: (i, 0)),
            _resident((1, d), lambda i: (0, 0)),
            _resident((1, d), lambda i: (0, 0)),
            _resident((d, d), lambda i: (0, 0)),
            _resident((1, d), lambda i: (0, 0)),
        ],
        out_specs=pl.BlockSpec((tm, d), lambda i: (i, 0)),
        out_shape=jax.ShapeDtypeStruct((m, d), F32),
        compiler_params=pltpu.CompilerParams(
            dimension_semantics=("arbitrary",),
            vmem_limit_bytes=_vmem_limit(d * d * 2 + 12 * tm * d * 4 + (4 << 20))),
        name="conf_out",
    )(y, x, lg, lb, w2, b2)


def _ffn_kernel(x_ref, g_ref, wg_ref, wu_ref, wd_ref, gn_ref, o_ref, *rest, final_norm):
    xn_ref = rest[-1]
    j = pl.program_id(1)

    @pl.when(j == 0)
    def _():
        x = x_ref[...]
        xn_ref[...] = _rmsnorm_bf16(x, g_ref[...])
        o_ref[...] = x

    xn = xn_ref[...]
    gate = jnp.dot(xn, wg_ref[...], preferred_element_type=F32)
    up = jnp.dot(xn, wu_ref[...], preferred_element_type=F32)
    act = (gate * jax.nn.sigmoid(gate) * up).astype(BF16)
    o_ref[...] += jnp.dot(act, wd_ref[...], preferred_element_type=F32)

    @pl.when(j == pl.num_programs(1) - 1)
    def _():
        h = o_ref[...]
        ms = jnp.mean(h * h, axis=-1, keepdims=True)
        hn = h * lax.rsqrt(ms + NORM_EPS) * gn_ref[...]
        if final_norm:
            o_ref[...] = hn
        else:
            rest[0][...] = hn.astype(BF16)


def _ffn(x, g, wg, wu, wd, gn, *, layer, tm, tf, final_norm):
    m, d = x.shape
    f = wg.shape[-1]
    row_spec = pl.BlockSpec((tm, d), lambda i, j: (i, 0))
    out_specs, out_shape = row_spec, jax.ShapeDtypeStruct((m, d), F32)
    if not final_norm:
        out_specs = [row_spec, row_spec]
        out_shape = [out_shape, jax.ShapeDtypeStruct((m, d), BF16)]
    return pl.pallas_call(
        functools.partial(_ffn_kernel, final_norm=final_norm),
        grid=(m // tm, f // tf),
        in_specs=[
            row_spec,
            _resident((1, d), lambda i, j: (0, 0)),
            pl.BlockSpec((None, d, tf), lambda i, j: (layer, 0, j)),
            pl.BlockSpec((None, d, tf), lambda i, j: (layer, 0, j)),
            pl.BlockSpec((None, tf, d), lambda i, j: (layer, j, 0)),
            _resident((1, d), lambda i, j: (0, 0)),
        ],
        out_specs=out_specs,
        out_shape=out_shape,
        scratch_shapes=[pltpu.VMEM((tm, d), BF16)],
        compiler_params=pltpu.CompilerParams(
            dimension_semantics=("arbitrary", "arbitrary"),
            vmem_limit_bytes=_vmem_limit(5 * tm * d * 4 + 3 * tm * d * 2 + 12 * d * tf + 5 * tm * tf * 4 + (4 << 20))),
        name="ffn_final" if final_norm else "ffn",
    )(x, g, wg, wu, wd, gn)


def _hy_mix_kernel(xn_ref, w0_ref, w1_ref, w2_ref, bi0_ref, bi1_ref, bi2_ref,
                   ws0_ref, ws1_ref, ws2_ref, bs0_ref, bs1_ref, bs2_ref,
                   g_ref, skip_ref, fwd_ref, inv_ref, o_ref, p_ref, x0_ref, ph_ref, yh_ref, *, nb, blk):
    seq = xn_ref.shape[1]
    xn = xn_ref[0]
    row = lax.broadcasted_iota(jnp.int32, (seq, 1), 0)
    first = row == 0
    last = row == seq - 1

    def branch(w_ref, bi_ref, ws_ref, bs_ref):
        z = jnp.dot(xn, w_ref[...], preferred_element_type=F32) + bi_ref[...]
        zp = jnp.where(first, 0.0, pltpu.roll(z, 1, axis=0))
        zn = jnp.where(last, 0.0, pltpu.roll(z, seq - 1, axis=0))
        return ws_ref[0:1, :] * zp + ws_ref[1:2, :] * z + ws_ref[2:3, :] * zn + bs_ref[...]

    x1 = branch(w1_ref, bi1_ref, ws1_ref, bs1_ref)
    v = branch(w2_ref, bi2_ref, ws2_ref, bs2_ref)
    p_ref[...] = v * x1
    x0_ref[...] = branch(w0_ref, bi0_ref, ws0_ref, bs0_ref)

    for b in range(nb):
        pb = p_ref[b * blk:(b + 1) * blk, :].astype(BF16)
        ph_ref[b] = jnp.dot(fwd_ref[...], pb, preferred_element_type=F32)
    for a in range(nb):
        _spectral_mix(g_ref, ph_ref, yh_ref, a, nb=nb, blk=blk, lag_blocks=nb - 1)
        y = jnp.dot(inv_ref[...], yh_ref[...], preferred_element_type=F32)
        rows = slice(a * blk, (a + 1) * blk)
        y = y + p_ref[rows, :] * skip_ref[...]
        o_ref[0, rows, :] = (y * x0_ref[rows, :]).astype(o_ref.dtype)


def _hy_mix(xn, w_in, b_in, w_short, b_short, gspec, skip, fwd, inv, *, tn, nb, blk):
    b, seq, d = xn.shape
    nt = d // tn

    def col(k):
        return lambda j, i: (0, k * nt + j)

    w_specs = [_resident((d, tn), col(k)) for k in range(3)]
    bi_specs = [pl.BlockSpec((1, tn), col(k)) for k in range(3)]
    ws_specs = [pl.BlockSpec((w_short.shape[0], tn), col(k)) for k in range(3)]
    bs_specs = [pl.BlockSpec((1, tn), col(k)) for k in range(3)]
    return pl.pallas_call(
        functools.partial(_hy_mix_kernel, nb=nb, blk=blk),
        grid=(nt, b),
        in_specs=[pl.BlockSpec((1, seq, d), lambda j, i: (i, 0, 0))]
        + w_specs + bi_specs + ws_specs + bs_specs
        + [_resident((2 * nb - 1, 2 * blk, tn), lambda j, i: (0, 0, j)),
           pl.BlockSpec((1, tn), lambda j, i: (0, j)),
           _resident(fwd.shape, lambda j, i: (0, 0)),
           _resident(inv.shape, lambda j, i: (0, 0))],
        out_specs=pl.BlockSpec((1, seq, tn), lambda j, i: (i, 0, j)),
        out_shape=jax.ShapeDtypeStruct((b, seq, d), BF16),
        scratch_shapes=[pltpu.VMEM((seq, tn), F32), pltpu.VMEM((seq, tn), F32),
                        pltpu.VMEM((nb, 2 * blk, tn), F32), pltpu.VMEM((2 * blk, tn), BF16)],
        compiler_params=pltpu.CompilerParams(
            dimension_semantics=("arbitrary", "arbitrary"),
            vmem_limit_bytes=_vmem_limit(V7X_VMEM_BYTES)),
        name="hy_mix",
    )(xn, w_in, w_in, w_in, b_in, b_in, b_in, w_short, w_short, w_short, b_short, b_short, b_short,
      gspec, skip, fwd, inv)


def _filter_kernel(z_ref, w1_ref, b1_ref, f1_ref, w2_ref, b2_ref, f2_ref, w3_ref, b3_ref, f3_ref,
                   w4f_ref, w4b_ref, df_ref, fwd_ref, eneg_ref, g_ref, hf_ref, *, nb, blk):
    seq = z_ref.shape[0]

    @pl.when(pl.program_id(0) == 0)
    def _():
        hf = jnp.sin(f1_ref[...] * (jnp.dot(z_ref[...], w1_ref[...], preferred_element_type=F32) + b1_ref[...]))
        hf = jnp.sin(f2_ref[...] * (jnp.dot(hf, w2_ref[...], preferred_element_type=F32) + b2_ref[...]))
        hf_ref[...] = jnp.sin(f3_ref[...] * (jnp.dot(hf, w3_ref[...], preferred_element_type=F32) + b3_ref[...]))

    hf = hf_ref[...]
    row = lax.broadcasted_iota(jnp.int32, (seq, 1), 0)
    t = row.astype(F32) * (1.0 / (seq - 1))
    decay = jnp.exp(-t * df_ref[...])
    kf = jnp.dot(hf, w4f_ref[...], preferred_element_type=F32) * decay
    kb = jnp.dot(hf, w4b_ref[...], preferred_element_type=F32) * decay
    kb = jnp.where(row == 0, 0.0, kb)

    fwd = fwd_ref[...]
    eneg = eneg_ref[...]

    def spectra(k):
        pos, neg = [], []
        for jb in range(nb):
            kj = k[jb * blk:(jb + 1) * blk, :].astype(BF16)
            pos.append(jnp.dot(fwd, kj, preferred_element_type=F32))
            neg.append(jnp.dot(eneg, kj, preferred_element_type=F32) if jb < nb - 1 else None)
        return pos, neg

    fpos, fneg = spectra(kf)
    bpos, bneg = spectra(kb)

    def store(idx, val, conj_add=None):
        re, im = val[:blk], val[blk:]
        if conj_add is not None:
            re, im = re + conj_add[:blk], im - conj_add[blk:]
        g_ref[idx, 0:blk, :] = re
        g_ref[idx, blk:, :] = im

    store(nb - 1, fpos[0], conj_add=bpos[0])
    for c in range(1, nb):
        store(nb - 1 + c, fneg[c - 1] + fpos[c])
        gb = bneg[c - 1] + bpos[c]
        g_ref[nb - 1 - c, 0:blk, :] = gb[:blk]
        g_ref[nb - 1 - c, blk:, :] = -gb[blk:]


def _filter_spectra(z, w1, b1, f1, w2, b2, f2, w3, b3, f3, w4, deltas, fwd, eneg, *, td, nb, blk):
    seq = z.shape[0]
    d = deltas.shape[1]
    nt = d // td
    order = w2.shape[0]
    full = lambda a: _resident(a.shape, lambda j: (0,) * a.ndim)
    return pl.pallas_call(
        functools.partial(_filter_kernel, nb=nb, blk=blk),
        grid=(nt,),
        in_specs=[full(z), full(w1), full(b1), full(f1), full(w2), full(b2), full(f2), full(w3), full(b3), full(f3),
                  pl.BlockSpec((order, td), lambda j: (0, j)),
                  pl.BlockSpec((order, td), lambda j: (0, nt + j)),
                  pl.BlockSpec((1, td), lambda j: (0, j)),
                  full(fwd), full(eneg)],
        out_specs=pl.BlockSpec((2 * nb - 1, 2 * blk, td), lambda j: (0, 0, j)),
        out_shape=jax.ShapeDtypeStruct((2 * nb - 1, 2 * blk, d), F32),
        scratch_shapes=[pltpu.VMEM((seq, order), F32)],
        compiler_params=pltpu.CompilerParams(
            dimension_semantics=("arbitrary",),
            vmem_limit_bytes=_vmem_limit(2 * (2 * nb - 1) * 2 * blk * td * 4 + (4 * nb) * 2 * blk * td * 4
                                         + 6 * seq * td * 4 + (8 << 20))),
        name="hy_filter",
    )(z, w1, b1, f1, w2, b2, f2, w3, b3, f3, w4, w4, deltas, fwd, eneg)


def _proj_res_kernel(a_ref, x_ref, w_ref, b_ref, o_ref):
    o_ref[...] = x_ref[...] + jnp.dot(a_ref[...], w_ref[...], preferred_element_type=F32) + b_ref[...]


def _proj_res(a, x, w, b, *, tm):
    m, d = x.shape
    return pl.pallas_call(
        _proj_res_kernel,
        grid=(m // tm,),
        in_specs=[pl.BlockSpec((tm, d), lambda i: (i, 0)),
                  pl.BlockSpec((tm, d), lambda i: (i, 0)),
                  _resident((d, d), lambda i: (0, 0)),
                  _resident((1, d), lambda i: (0, 0))],
        out_specs=pl.BlockSpec((tm, d), lambda i: (i, 0)),
        out_shape=jax.ShapeDtypeStruct((m, d), F32),
        compiler_params=pltpu.CompilerParams(
            dimension_semantics=("arbitrary",),
            vmem_limit_bytes=_vmem_limit(d * d * 2 + 7 * tm * d * 4 + (4 << 20))),
        name="hy_out",
    )(a, x, w, b)


def _dft_angles(blk):
    f = (np.arange(blk, dtype=np.float64) + 0.5)[:, None]
    return f * (2.0 * np.pi / (2 * blk))


def _dft_tables(blk):
    w = _dft_angles(blk)
    q = np.arange(blk, dtype=np.float64)[None, :]
    ang = w * q
    fwd = np.concatenate([np.cos(ang), -np.sin(ang)], axis=0)
    ang_n = w * (q - blk)
    eneg = np.concatenate([np.cos(ang_n), -np.sin(ang_n)], axis=0)
    eneg[:, 0] = 0.0
    inv = np.concatenate([np.cos(ang).T, -np.sin(ang).T], axis=1) / blk
    return (jnp.asarray(fwd, dtype=BF16), jnp.asarray(eneg, dtype=BF16), jnp.asarray(inv, dtype=BF16))


def _dw_tables(blk, width, taps_pad):
    w = _dft_angles(blk)
    pad = (width - 1) // 2
    tab = np.zeros((3, 2 * blk, taps_pad), dtype=np.float64)
    for ci, c in enumerate((-1, 0, 1)):
        for j in range(width):
            e = pad - j - blk * c
            if abs(e) <= blk - 1:
                tab[ci, :blk, j] = np.cos(w[:, 0] * e)
                tab[ci, blk:, j] = -np.sin(w[:, 0] * e)
    return jnp.asarray(tab, dtype=F32)


def _hyena_positional_features(seq, pad_to):
    t = jnp.linspace(0.0, 1.0, seq, dtype=F32)[:, None]
    bands = jnp.linspace(1e-4, HYENA_N_BANDS - 1, HYENA_N_BANDS, dtype=F32)[None, :]
    wpos = (2.0 * math.pi) * jnp.arange(seq, dtype=F32)[:, None] / seq
    z = jnp.concatenate([t, jnp.cos(bands * wpos), -jnp.sin(bands * wpos)], axis=-1)
    return jnp.pad(z, ((0, 0), (0, pad_to - z.shape[1])))


def _hyena_decay_rates(d):
    max_decay = math.log(HYENA_DECAY_TARGET) / HYENA_FAST_DECAY_PCT
    min_decay = math.log(HYENA_DECAY_TARGET) / HYENA_SLOW_DECAY_PCT
    return jnp.abs(jnp.linspace(min_decay, max_decay, d, dtype=F32))[None, :]


def _pick(n, prefs):
    for p in prefs:
        if n % p == 0:
            return p
    return n


def kernel(x, norm_mix, norm_ffn, cv_w_pw1, cv_b_pw1, cv_w_dw, cv_b_dw, cv_ln_g, cv_ln_b, cv_w_pw2, cv_b_pw2, hy_w_in, hy_b_in, hy_w_short, hy_b_short, hy_f_w1, hy_f_b1, hy_f_freq1, hy_f_w2, hy_f_b2, hy_f_freq2, hy_f_w3, hy_f_b3, hy_f_freq3, hy_f_w4, hy_skip, hy_w_out, hy_b_out, ffn_w_gate, ffn_w_up, ffn_w_down, norm_final):
    bsz, seq, d = x.shape
    m = bsz * seq
    f = ffn_w_gate.shape[-1]
    depth = norm_mix.shape[0]
    assert depth == 2 and cv_w_pw1.shape[0] == 1 and hy_w_in.shape[0] == 1

    tm = _pick(m, (512, 256, 128))
    tf = _pick(f, (512, 256, 128))
    tn_col = _pick(d, (256, 128))
    td = _pick(d, (512, 256, 128))
    blk = _pick(seq, (CONV_BLOCK, 256, 128))
    nb = seq // blk
    dw_blk = _pick(seq, (DW_BLOCK, 128))
    width = cv_w_dw.shape[1]
    assert (width - 1) // 2 < dw_blk

    row = lambda a: a.reshape(1, -1).astype(F32)
    bf = lambda a: a.astype(BF16)

    h = x.reshape(m, d)

    taps_pad = -(-width // V7X_SUBLANES) * V7X_SUBLANES
    wdw = jnp.pad(cv_w_dw[0], ((0, taps_pad - width), (0, 0)))
    dw_fwd, _, dw_inv = _dft_tables(dw_blk)
    dw_spec = _dw_spectra(_dw_tables(dw_blk, width, taps_pad), wdw, td=td)
    y = _conf_in(x, row(norm_mix[0]), bf(cv_w_pw1[0]), row(cv_b_pw1[0]), dw_spec, row(cv_b_dw[0]),
                 dw_fwd, dw_inv, tn=tn_col, blk=dw_blk)
    h = _conf_out(y.reshape(m, d), h, row(cv_ln_g[0]), row(cv_ln_b[0]), bf(cv_w_pw2[0]), row(cv_b_pw2[0]), tm=tm)
    wg, wu, wd = bf(ffn_w_gate), bf(ffn_w_up), bf(ffn_w_down)
    h, hn = _ffn(h, row(norm_ffn[0]), wg, wu, wd, row(norm_mix[1]), layer=0, tm=tm, tf=tf, final_norm=False)

    fwd, eneg, inv = _dft_tables(blk)
    emb = hy_f_w1.shape[1]
    emb_pad = -(-emb // V7X_LANES) * V7X_LANES
    z = _hyena_positional_features(seq, emb_pad)
    w1 = jnp.pad(hy_f_w1[0], ((0, emb_pad - emb), (0, 0)))
    gspec = _filter_spectra(z, w1, row(hy_f_b1[0]), row(hy_f_freq1[0]), hy_f_w2[0], row(hy_f_b2[0]),
                            row(hy_f_freq2[0]), hy_f_w3[0], row(hy_f_b3[0]), row(hy_f_freq3[0]), hy_f_w4[0],
                            _hyena_decay_rates(d), fwd, eneg, td=td, nb=nb, blk=blk)
    gated = _hy_mix(hn.reshape(bsz, seq, d), bf(hy_w_in[0]), row(hy_b_in[0]), hy_w_short[0], row(hy_b_short[0]),
                    gspec, row(hy_skip[0]), fwd, inv, tn=tn_col, nb=nb, blk=blk)
    h = _proj_res(gated.reshape(m, d), h, bf(hy_w_out[0]), row(hy_b_out[0]), tm=tm)
    h = _ffn(h, row(norm_ffn[1]), wg, wu, wd, row(norm_final), layer=1, tm=tm, tf=tf, final_norm=True)
    return h.reshape(bsz, seq, d)
```

```python
import functools
import math

import jax
import jax.numpy as jnp
import numpy as np
from jax import lax
from jax.experimental import pallas as pl
from jax.experimental.pallas import tpu as pltpu

NORM_EPS = 1e-6
LN_EPS = 1e-5
HYENA_N_BANDS = 16
HYENA_FAST_DECAY_PCT = 0.3
HYENA_SLOW_DECAY_PCT = 1.5
HYENA_DECAY_TARGET = 1e-2

V7X_LANES = 128
V7X_SUBLANES = 8
V7X_BF16_SUBLANES = 16
V7X_VMEM_BYTES = 64 * 1024 * 1024

DW_BLOCK = 256
CONV_BLOCK = 512

F32 = jnp.float32
BF16 = jnp.bfloat16


def _vmem_limit(nbytes):
    return int(min(max(nbytes, 32 * 1024 * 1024), V7X_VMEM_BYTES - 4 * 1024 * 1024))


def _resident(block_shape, index_map):
    return pl.BlockSpec(block_shape, index_map, pipeline_mode=pl.Buffered(1))


def _rmsnorm_bf16(x, g):
    ms = jnp.mean(x * x, axis=-1, keepdims=True)
    return (x * lax.rsqrt(ms + NORM_EPS) * g).astype(BF16)


def _spectral_mix(g_ref, ph_ref, yh_ref, *, nb, blk, lag_blocks):
    lanes = yh_ref.shape[-1]
    slab = V7X_BF16_SUBLANES
    for s in range(blk // slab):
        r_re = slice(s * slab, (s + 1) * slab)
        r_im = slice(blk + s * slab, blk + (s + 1) * slab)
        for lc in range(lanes // V7X_LANES):
            ls = slice(lc * V7X_LANES, (lc + 1) * V7X_LANES)
            acc = [None] * nb
            for b in range(nb):
                pre, pim = ph_ref[b, r_re, ls], ph_ref[b, r_im, ls]
                for a in range(max(0, b - lag_blocks), min(nb, b + lag_blocks + 1)):
                    gi = a - b + lag_blocks
                    gre, gim = g_ref[gi, r_re, ls], g_ref[gi, r_im, ls]
                    t_re = gre * pre - gim * pim
                    t_im = gre * pim + gim * pre
                    acc[a] = (t_re, t_im) if acc[a] is None else (acc[a][0] + t_re, acc[a][1] + t_im)
                    if b == min(nb, a + lag_blocks + 1) - 1:
                        yh_ref[a, r_re, ls] = acc[a][0].astype(yh_ref.dtype)
                        yh_ref[a, r_im, ls] = acc[a][1].astype(yh_ref.dtype)
                        acc[a] = None


def _dw_spectra_kernel(e_ref, w_ref, o_ref):
    for c in range(e_ref.shape[0]):
        o_ref[c] = jnp.dot(e_ref[c], w_ref[...], preferred_element_type=F32)


def _dw_spectra(etab, wdw, *, td):
    nc, rows, taps = etab.shape
    d = wdw.shape[1]
    return pl.pallas_call(
        _dw_spectra_kernel,
        grid=(d // td,),
        in_specs=[_resident(etab.shape, lambda j: (0, 0, 0)), pl.BlockSpec((taps, td), lambda j: (0, j))],
        out_specs=pl.BlockSpec((nc, rows, td), lambda j: (0, 0, j)),
        out_shape=jax.ShapeDtypeStruct((nc, rows, d), F32),
        compiler_params=pltpu.CompilerParams(dimension_semantics=("arbitrary",)),
        name="dw_spectra",
    )(etab, wdw)


def _conf_in_kernel(x_ref, g_ref, wa_ref, wb_ref, ba_ref, bb_ref, gs_ref, bdw_ref, fwd_ref, inv_ref,
                    y_ref, xn_ref, ph_ref, yh_ref, *, nb, blk, nt):
    t = pl.program_id(0)
    n_chunks = pl.num_programs(0) - 1

    @pl.when(t == 0)
    def _():
        ph_ref[...] = jnp.zeros(ph_ref.shape, ph_ref.dtype)

    @pl.when(jnp.logical_and(t % nt == 0, t < n_chunks))
    def _():
        xn_ref[...] = _rmsnorm_bf16(x_ref[0], g_ref[...])

    _spectral_mix(gs_ref, ph_ref, yh_ref, nb=nb, blk=blk, lag_blocks=1)
    for a in range(nb):
        y = jnp.dot(inv_ref[...], yh_ref[a], preferred_element_type=F32)
        y_ref[0, a * blk:(a + 1) * blk, :] = y + bdw_ref[...]

    xn = xn_ref[...]
    a1 = jnp.dot(xn, wa_ref[...], preferred_element_type=F32) + ba_ref[...]
    a2 = jnp.dot(xn, wb_ref[...], preferred_element_type=F32) + bb_ref[...]
    u = (a1 * jax.nn.sigmoid(a2)).astype(BF16)
    for b in range(nb):
        ph_ref[b] = jnp.dot(fwd_ref[...], u[b * blk:(b + 1) * blk, :], preferred_element_type=F32)


def _conf_in(x, g, w, b, gs, bdw, fwd, inv, *, tn, blk):
    bsz, seq, d = x.shape
    nt = d // tn
    nb = seq // blk
    n_chunks = bsz * nt

    def front(t):
        tf = jnp.minimum(t, n_chunks - 1)
        return tf // nt, tf % nt

    def back(t):
        tb = jnp.maximum(t - 1, 0)
        return tb // nt, tb % nt

    return pl.pallas_call(
        functools.partial(_conf_in_kernel, nb=nb, blk=blk, nt=nt),
        grid=(n_chunks + 1,),
        in_specs=[
            _resident((1, seq, d), lambda t: (front(t)[0], 0, 0)),
            _resident((1, d), lambda t: (0, 0)),
            pl.BlockSpec((d, tn), lambda t: (0, front(t)[1])),
            pl.BlockSpec((d, tn), lambda t: (0, nt + front(t)[1])),
            pl.BlockSpec((1, tn), lambda t: (0, front(t)[1])),
            pl.BlockSpec((1, tn), lambda t: (0, nt + front(t)[1])),
            pl.BlockSpec((gs.shape[0], 2 * blk, tn), lambda t: (0, 0, back(t)[1])),
            pl.BlockSpec((1, tn), lambda t: (0, back(t)[1])),
            _resident(fwd.shape, lambda t: (0, 0)),
            _resident(inv.shape, lambda t: (0, 0)),
        ],
        out_specs=pl.BlockSpec((1, seq, tn), lambda t: (back(t)[0], 0, back(t)[1])),
        out_shape=jax.ShapeDtypeStruct((bsz, seq, d), F32),
        scratch_shapes=[pltpu.VMEM((seq, d), BF16), pltpu.VMEM((nb, 2 * blk, tn), F32),
                        pltpu.VMEM((nb, 2 * blk, tn), BF16)],
        compiler_params=pltpu.CompilerParams(
            dimension_semantics=("arbitrary",),
            vmem_limit_bytes=_vmem_limit(seq * d * 6 + 8 * d * tn + 16 * seq * tn * 4 + (8 << 20))),
        name="conf_in",
    )(x, g, w, w, b, b, gs, bdw, fwd, inv)


def _conf_out_kernel(y_ref, x_ref, lg_ref, lb_ref, w2_ref, b2_ref, o_ref):
    y = y_ref[...]
    mu = jnp.mean(y, axis=-1, keepdims=True)
    yc = y - mu
    var = jnp.mean(yc * yc, axis=-1, keepdims=True)
    z = yc * lax.rsqrt(var + LN_EPS) * lg_ref[...] + lb_ref[...]
    act = (z * jax.nn.sigmoid(z)).astype(BF16)
    o_ref[...] = x_ref[...] + jnp.dot(act, w2_ref[...], preferred_element_type=F32) + b2_ref[...]


def _conf_out(y, x, lg, lb, w2, b2, *, tm):
    m, d = y.shape
    return pl.pallas_call(
        _conf_out_kernel,
        grid=(m // tm,),
        in_specs=[
            pl.BlockSpec((tm, d), lambda i: (i, 0)),
            pl.BlockSpec((tm, d), lambda i: (i, 0)),
            _resident((1, d), lambda i: (0, 0)),
            _resident((1, d), lambda i: (0, 0)),
            _resident((d, d), lambda i: (0, 0)),
            _resident((1, d), lambda i: (0, 0)),
        ],
        out_specs=pl.BlockSpec((tm, d), lambda i: (i, 0)),
        out_shape=jax.ShapeDtypeStruct((m, d), F32),
        compiler_params=pltpu.CompilerParams(
            dimension_semantics=("arbitrary",),
            vmem_limit_bytes=_vmem_limit(d * d * 2 + 12 * tm * d * 4 + (4 << 20))),
        name="conf_out",
    )(y, x, lg, lb, w2, b2)


def _ffn_kernel(x_ref, g_ref, wg_ref, wu_ref, wd_ref, gn_ref, o_ref, *rest, final_norm):
    xn_ref = rest[-1]
    j = pl.program_id(1)

    @pl.when(j == 0)
    def _():
        x = x_ref[...]
        xn_ref[...] = _rmsnorm_bf16(x, g_ref[...])
        o_ref[...] = x

    xn = xn_ref[...]
    gate = jnp.dot(xn, wg_ref[...], preferred_element_type=F32)
    up = jnp.dot(xn, wu_ref[...], preferred_element_type=F32)
    act = (gate * jax.nn.sigmoid(gate) * up).astype(BF16)
    o_ref[...] += jnp.dot(act, wd_ref[...], preferred_element_type=F32)

    @pl.when(j == pl.num_programs(1) - 1)
    def _():
        h = o_ref[...]
        ms = jnp.mean(h * h, axis=-1, keepdims=True)
        hn = h * lax.rsqrt(ms + NORM_EPS) * gn_ref[...]
        if final_norm:
            o_ref[...] = hn
        else:
            rest[0][...] = hn.astype(BF16)


def _ffn(x, g, wg, wu, wd, gn, *, layer, tm, tf, final_norm):
    m, d = x.shape
    f = wg.shape[-1]
    row_spec = pl.BlockSpec((tm, d), lambda i, j: (i, 0))
    out_specs, out_shape = row_spec, jax.ShapeDtypeStruct((m, d), F32)
    if not final_norm:
        out_specs = [row_spec, row_spec]
        out_shape = [out_shape, jax.ShapeDtypeStruct((m, d), BF16)]
    return pl.pallas_call(
        functools.partial(_ffn_kernel, final_norm=final_norm),
        grid=(m // tm, f // tf),
        in_specs=[
            row_spec,
            _resident((1, d), lambda i, j: (0, 0)),
            pl.BlockSpec((None, d, tf), lambda i, j: (layer, 0, j)),
            pl.BlockSpec((None, d, tf), lambda i, j: (layer, 0, j)),
            pl.BlockSpec((None, tf, d), lambda i, j: (layer, j, 0)),
            _resident((1, d), lambda i, j: (0, 0)),
        ],
        out_specs=out_specs,
        out_shape=out_shape,
        scratch_shapes=[pltpu.VMEM((tm, d), BF16)],
        compiler_params=pltpu.CompilerParams(
            dimension_semantics=("arbitrary", "arbitrary"),
            vmem_limit_bytes=_vmem_limit(5 * tm * d * 4 + 3 * tm * d * 2 + 12 * d * tf + 5 * tm * tf * 4 + (4 << 20))),
        name="ffn_final" if final_norm else "ffn",
    )(x, g, wg, wu, wd, gn)


def _hy_mix_kernel(xn_ref, w0_ref, w1_ref, w2_ref, bi0_ref, bi1_ref, bi2_ref,
                   ws0_ref, ws1_ref, ws2_ref, bs0_ref, bs1_ref, bs2_ref,
                   g_ref, fwd_ref, inv_ref, o_ref, p_ref, x0_ref, ph_ref, yh_ref, *, nb, blk):
    seq = xn_ref.shape[1]

    @pl.when(pl.program_id(0) == 0)
    def _():
        ph_ref[...] = jnp.zeros(ph_ref.shape, ph_ref.dtype)
        x0_ref[...] = jnp.zeros(x0_ref.shape, x0_ref.dtype)

    _spectral_mix(g_ref, ph_ref, yh_ref, nb=nb, blk=blk, lag_blocks=nb - 1)
    for a in range(nb):
        rows = slice(a * blk, (a + 1) * blk)
        y = jnp.dot(inv_ref[...], yh_ref[a], preferred_element_type=F32)
        o_ref[0, rows, :] = (y * x0_ref[rows, :]).astype(o_ref.dtype)

    row = lax.broadcasted_iota(jnp.int32, (seq, 1), 0)
    first = row == 0
    last = row == seq - 1

    def branch(w_ref, bi_ref, ws_ref, bs_ref):
        z = jnp.dot(xn_ref[0], w_ref[...], preferred_element_type=F32) + bi_ref[...]
        zp = jnp.where(first, 0.0, pltpu.roll(z, 1, axis=0))
        zn = jnp.where(last, 0.0, pltpu.roll(z, seq - 1, axis=0))
        return ws_ref[0:1, :] * zp + ws_ref[1:2, :] * z + ws_ref[2:3, :] * zn + bs_ref[...]

    x1 = branch(w1_ref, bi1_ref, ws1_ref, bs1_ref)
    v = branch(w2_ref, bi2_ref, ws2_ref, bs2_ref)
    p_ref[...] = (v * x1).astype(BF16)
    x0_ref[...] = branch(w0_ref, bi0_ref, ws0_ref, bs0_ref)
    for b in range(nb):
        ph_ref[b] = jnp.dot(fwd_ref[...], p_ref[b * blk:(b + 1) * blk, :], preferred_element_type=F32)


def _hy_mix(xn, w_in, b_in, w_short, b_short, gspec, fwd, inv, *, tn, nb, blk):
    bsz, seq, d = xn.shape
    nt = d // tn
    n_items = nt * bsz

    def front(t):
        tf = jnp.minimum(t, n_items - 1)
        return tf // bsz, tf % bsz

    def back(t):
        tb = jnp.maximum(t - 1, 0)
        return tb // bsz, tb % bsz

    def col(k):
        return lambda t: (0, k * nt + front(t)[0])

    w_specs = [_resident((d, tn), col(k)) for k in range(3)]
    bi_specs = [pl.BlockSpec((1, tn), col(k)) for k in range(3)]
    ws_specs = [pl.BlockSpec((w_short.shape[0], tn), col(k)) for k in range(3)]
    bs_specs = [pl.BlockSpec((1, tn), col(k)) for k in range(3)]
    return pl.pallas_call(
        functools.partial(_hy_mix_kernel, nb=nb, blk=blk),
        grid=(n_items + 1,),
        in_specs=[pl.BlockSpec((1, seq, d), lambda t: (front(t)[1], 0, 0))]
        + w_specs + bi_specs + ws_specs + bs_specs
        + [_resident((2 * nb - 1, 2 * blk, tn), lambda t: (0, 0, back(t)[0])),
           _resident(fwd.shape, lambda t: (0, 0)),
           _resident(inv.shape, lambda t: (0, 0))],
        out_specs=pl.BlockSpec((1, seq, tn), lambda t: (back(t)[1], 0, back(t)[0])),
        out_shape=jax.ShapeDtypeStruct((bsz, seq, d), BF16),
        scratch_shapes=[pltpu.VMEM((seq, tn), BF16), pltpu.VMEM((seq, tn), F32),
                        pltpu.VMEM((nb, 2 * blk, tn), F32), pltpu.VMEM((nb, 2 * blk, tn), BF16)],
        compiler_params=pltpu.CompilerParams(
            dimension_semantics=("arbitrary",),
            vmem_limit_bytes=_vmem_limit(V7X_VMEM_BYTES)),
        name="hy_mix",
    )(xn, w_in, w_in, w_in, b_in, b_in, b_in, w_short, w_short, w_short, b_short, b_short, b_short,
      gspec, fwd, inv)


def _filter_kernel(z_ref, w1_ref, b1_ref, f1_ref, w2_ref, b2_ref, f2_ref, w3_ref, b3_ref, f3_ref,
                   w4f_ref, w4b_ref, df_ref, skip_ref, fwd_ref, eneg_ref, g_ref, hf_ref, *, nb, blk):
    seq = z_ref.shape[0]

    @pl.when(pl.program_id(0) == 0)
    def _():
        hf = jnp.sin(f1_ref[...] * (jnp.dot(z_ref[...], w1_ref[...], preferred_element_type=F32) + b1_ref[...]))
        hf = jnp.sin(f2_ref[...] * (jnp.dot(hf, w2_ref[...], preferred_element_type=F32) + b2_ref[...]))
        hf_ref[...] = jnp.sin(f3_ref[...] * (jnp.dot(hf, w3_ref[...], preferred_element_type=F32) + b3_ref[...]))

    hf = hf_ref[...]
    row = lax.broadcasted_iota(jnp.int32, (seq, 1), 0)
    t = row.astype(F32) * (1.0 / (seq - 1))
    decay = jnp.exp(-t * df_ref[...])
    kf = jnp.dot(hf, w4f_ref[...], preferred_element_type=F32) * decay
    kb = jnp.dot(hf, w4b_ref[...], preferred_element_type=F32) * decay
    kb = jnp.where(row == 0, 0.0, kb)

    fwd = fwd_ref[...]
    eneg = eneg_ref[...]

    def spectra(k):
        pos, neg = [], []
        for jb in range(nb):
            kj = k[jb * blk:(jb + 1) * blk, :].astype(BF16)
            pos.append(jnp.dot(fwd, kj, preferred_element_type=F32))
            neg.append(jnp.dot(eneg, kj, preferred_element_type=F32) if jb < nb - 1 else None)
        return pos, neg

    fpos, fneg = spectra(kf)
    bpos, bneg = spectra(kb)

    def store(idx, val, conj_add=None):
        re, im = val[:blk], val[blk:]
        if conj_add is not None:
            re, im = re + conj_add[:blk] + skip_ref[...], im - conj_add[blk:]
        g_ref[idx, 0:blk, :] = re
        g_ref[idx, blk:, :] = im

    store(nb - 1, fpos[0], conj_add=bpos[0])
    for c in range(1, nb):
        store(nb - 1 + c, fneg[c - 1] + fpos[c])
        gb = bneg[c - 1] + bpos[c]
        g_ref[nb - 1 - c, 0:blk, :] = gb[:blk]
        g_ref[nb - 1 - c, blk:, :] = -gb[blk:]


def _filter_spectra(z, w1, b1, f1, w2, b2, f2, w3, b3, f3, w4, deltas, skip, fwd, eneg, *, td, nb, blk):
    seq = z.shape[0]
    d = deltas.shape[1]
    nt = d // td
    order = w2.shape[0]
    full = lambda a: _resident(a.shape, lambda j: (0,) * a.ndim)
    return pl.pallas_call(
        functools.partial(_filter_kernel, nb=nb, blk=blk),
        grid=(nt,),
        in_specs=[full(z), full(w1), full(b1), full(f1), full(w2), full(b2), full(f2), full(w3), full(b3), full(f3),
                  pl.BlockSpec((order, td), lambda j: (0, j)),
                  pl.BlockSpec((order, td), lambda j: (0, nt + j)),
                  pl.BlockSpec((1, td), lambda j: (0, j)),
                  pl.BlockSpec((1, td), lambda j: (0, j)),
                  full(fwd), full(eneg)],
        out_specs=pl.BlockSpec((2 * nb - 1, 2 * blk, td), lambda j: (0, 0, j)),
        out_shape=jax.ShapeDtypeStruct((2 * nb - 1, 2 * blk, d), F32),
        scratch_shapes=[pltpu.VMEM((seq, order), F32)],
        compiler_params=pltpu.CompilerParams(
            dimension_semantics=("arbitrary",),
            vmem_limit_bytes=_vmem_limit(2 * (2 * nb - 1) * 2 * blk * td * 4 + (4 * nb) * 2 * blk * td * 4
                                         + 6 * seq * td * 4 + (8 << 20))),
        name="hy_filter",
    )(z, w1, b1, f1, w2, b2, f2, w3, b3, f3, w4, w4, deltas, skip, fwd, eneg)


def _proj_res_kernel(a_ref, x_ref, w_ref, b_ref, o_ref):
    o_ref[...] = x_ref[...] + jnp.dot(a_ref[...], w_ref[...], preferred_element_type=F32) + b_ref[...]


def _proj_res(a, x, w, b, *, tm):
    m, d = x.shape
    return pl.pallas_call(
        _proj_res_kernel,
        grid=(m // tm,),
        in_specs=[pl.BlockSpec((tm, d), lambda i: (i, 0)),
                  pl.BlockSpec((tm, d), lambda i: (i, 0)),
                  _resident((d, d), lambda i: (0, 0)),
                  _resident((1, d), lambda i: (0, 0))],
        out_specs=pl.BlockSpec((tm, d), lambda i: (i, 0)),
        out_shape=jax.ShapeDtypeStruct((m, d), F32),
        compiler_params=pltpu.CompilerParams(
            dimension_semantics=("arbitrary",),
            vmem_limit_bytes=_vmem_limit(d * d * 2 + 7 * tm * d * 4 + (4 << 20))),
        name="hy_out",
    )(a, x, w, b)


def _dft_angles(blk):
    f = (np.arange(blk, dtype=np.float64) + 0.5)[:, None]
    return f * (2.0 * np.pi / (2 * blk))


def _dft_tables(blk):
    w = _dft_angles(blk)
    q = np.arange(blk, dtype=np.float64)[None, :]
    ang = w * q
    fwd = np.concatenate([np.cos(ang), -np.sin(ang)], axis=0)
    ang_n = w * (q - blk)
    eneg = np.concatenate([np.cos(ang_n), -np.sin(ang_n)], axis=0)
    eneg[:, 0] = 0.0
    inv = np.concatenate([np.cos(ang).T, -np.sin(ang).T], axis=1) / blk
    return (jnp.asarray(fwd, dtype=BF16), jnp.asarray(eneg, dtype=BF16), jnp.asarray(inv, dtype=BF16))


def _dw_tables(blk, width, taps_pad):
    w = _dft_angles(blk)
    pad = (width - 1) // 2
    tab = np.zeros((3, 2 * blk, taps_pad), dtype=np.float64)
    for ci, c in enumerate((-1, 0, 1)):
        for j in range(width):
            e = pad - j - blk * c
            if abs(e) <= blk - 1:
                tab[ci, :blk, j] = np.cos(w[:, 0] * e)
                tab[ci, blk:, j] = -np.sin(w[:, 0] * e)
    return jnp.asarray(tab, dtype=F32)


def _hyena_positional_features(seq, pad_to):
    t = jnp.linspace(0.0, 1.0, seq, dtype=F32)[:, None]
    bands = jnp.linspace(1e-4, HYENA_N_BANDS - 1, HYENA_N_BANDS, dtype=F32)[None, :]
    wpos = (2.0 * math.pi) * jnp.arange(seq, dtype=F32)[:, None] / seq
    z = jnp.concatenate([t, jnp.cos(bands * wpos), -jnp.sin(bands * wpos)], axis=-1)
    return jnp.pad(z, ((0, 0), (0, pad_to - z.shape[1])))


def _hyena_decay_rates(d):
    max_decay = math.log(HYENA_DECAY_TARGET) / HYENA_FAST_DECAY_PCT
    min_decay = math.log(HYENA_DECAY_TARGET) / HYENA_SLOW_DECAY_PCT
    return jnp.abs(jnp.linspace(min_decay, max_decay, d, dtype=F32))[None, :]


def _pick(n, prefs):
    for p in prefs:
        if n % p == 0:
            return p
    return n


def kernel(x, norm_mix, norm_ffn, cv_w_pw1, cv_b_pw1, cv_w_dw, cv_b_dw, cv_ln_g, cv_ln_b, cv_w_pw2, cv_b_pw2, hy_w_in, hy_b_in, hy_w_short, hy_b_short, hy_f_w1, hy_f_b1, hy_f_freq1, hy_f_w2, hy_f_b2, hy_f_freq2, hy_f_w3, hy_f_b3, hy_f_freq3, hy_f_w4, hy_skip, hy_w_out, hy_b_out, ffn_w_gate, ffn_w_up, ffn_w_down, norm_final):
    bsz, seq, d = x.shape
    m = bsz * seq
    f = ffn_w_gate.shape[-1]
    depth = norm_mix.shape[0]
    assert depth == 2 and cv_w_pw1.shape[0] == 1 and hy_w_in.shape[0] == 1

    tm = _pick(m, (512, 256, 128))
    tf = _pick(f, (512, 256, 128))
    tn_col = _pick(d, (256, 128))
    td = _pick(d, (512, 256, 128))
    blk = _pick(seq, (CONV_BLOCK, 256, 128))
    nb = seq // blk
    dw_blk = _pick(seq, (DW_BLOCK, 128))
    width = cv_w_dw.shape[1]
    assert (width - 1) // 2 < dw_blk

    row = lambda a: a.reshape(1, -1).astype(F32)
    bf = lambda a: a.astype(BF16)

    h = x.reshape(m, d)

    taps_pad = -(-width // V7X_SUBLANES) * V7X_SUBLANES
    wdw = jnp.pad(cv_w_dw[0], ((0, taps_pad - width), (0, 0)))
    dw_fwd, _, dw_inv = _dft_tables(dw_blk)
    dw_spec = _dw_spectra(_dw_tables(dw_blk, width, taps_pad), wdw, td=td)
    y = _conf_in(x, row(norm_mix[0]), bf(cv_w_pw1[0]), row(cv_b_pw1[0]), dw_spec, row(cv_b_dw[0]),
                 dw_fwd, dw_inv, tn=tn_col, blk=dw_blk)
    h = _conf_out(y.reshape(m, d), h, row(cv_ln_g[0]), row(cv_ln_b[0]), bf(cv_w_pw2[0]), row(cv_b_pw2[0]), tm=tm)
    wg, wu, wd = bf(ffn_w_gate), bf(ffn_w_up), bf(ffn_w_down)
    h, hn = _ffn(h, row(norm_ffn[0]), wg, wu, wd, row(norm_mix[1]), layer=0, tm=tm, tf=tf, final_norm=False)

    fwd, eneg, inv = _dft_tables(blk)
    emb = hy_f_w1.shape[1]
    emb_pad = -(-emb // V7X_LANES) * V7X_LANES
    z = _hyena_positional_features(seq, emb_pad)
    w1 = jnp.pad(hy_f_w1[0], ((0, emb_pad - emb), (0, 0)))
    gspec = _filter_spectra(z, w1, row(hy_f_b1[0]), row(hy_f_freq1[0]), hy_f_w2[0], row(hy_f_b2[0]),
                            row(hy_f_freq2[0]), hy_f_w3[0], row(hy_f_b3[0]), row(hy_f_freq3[0]), hy_f_w4[0],
                            _hyena_decay_rates(d), row(hy_skip[0]), fwd, eneg, td=td, nb=nb, blk=blk)
    gated = _hy_mix(hn.reshape(bsz, seq, d), bf(hy_w_in[0]), row(hy_b_in[0]), hy_w_short[0], row(hy_b_short[0]),
                    gspec, fwd, inv, tn=tn_col, nb=nb, blk=blk)
    h = _proj_res(gated.reshape(m, d), h, bf(hy_w_out[0]), row(hy_b_out[0]), tm=tm)
    h = _ffn(h, row(norm_ffn[1]), wg, wu, wd, row(norm_final), layer=1, tm=tm, tf=tf, final_norm=True)
    return h.reshape(bsz, seq, d)
```

```python
import functools
import math

import jax
import jax.numpy as jnp
import numpy as np
from jax import lax
from jax.experimental import pallas as pl
from jax.experimental.pallas import tpu as pltpu

NORM_EPS = 1e-6
LN_EPS = 1e-5
HYENA_N_BANDS = 16
HYENA_FAST_DECAY_PCT = 0.3
HYENA_SLOW_DECAY_PCT = 1.5
HYENA_DECAY_TARGET = 1e-2

V7X_LANES = 128
V7X_SUBLANES = 8
V7X_BF16_SUBLANES = 16
V7X_VMEM_BYTES = 64 * 1024 * 1024

DW_BLOCK = 256
CONV_BLOCK = 512

F32 = jnp.float32
BF16 = jnp.bfloat16


def _vmem_limit(nbytes):
    return int(min(max(nbytes, 32 * 1024 * 1024), V7X_VMEM_BYTES - 4 * 1024 * 1024))


def _resident(block_shape, index_map):
    return pl.BlockSpec(block_shape, index_map, pipeline_mode=pl.Buffered(1))


def _rmsnorm_bf16(x, g):
    ms = jnp.mean(x * x, axis=-1, keepdims=True)
    return (x * lax.rsqrt(ms + NORM_EPS) * g).astype(BF16)


def _spectral_mix(g_ref, ph_ref, yh_ref, *, nb, blk, lag_blocks, slabs=None, out_blocks=None):
    lanes = yh_ref.shape[-1]
    slab = V7X_BF16_SUBLANES
    outs = range(nb) if out_blocks is None else out_blocks
    for s in (range(blk // slab) if slabs is None else slabs):
        r_re = slice(s * slab, (s + 1) * slab)
        r_im = slice(blk + s * slab, blk + (s + 1) * slab)
        for lc in range(lanes // V7X_LANES):
            ls = slice(lc * V7X_LANES, (lc + 1) * V7X_LANES)
            acc = [None] * nb
            for b in range(nb):
                targets = [a for a in range(max(0, b - lag_blocks), min(nb, b + lag_blocks + 1)) if a in outs]
                if not targets:
                    continue
                pre, pim = ph_ref[b, r_re, ls], ph_ref[b, r_im, ls]
                for a in targets:
                    gi = a - b + lag_blocks
                    gre, gim = g_ref[gi, r_re, ls], g_ref[gi, r_im, ls]
                    t_re = gre * pre - gim * pim
                    t_im = gre * pim + gim * pre
                    acc[a] = (t_re, t_im) if acc[a] is None else (acc[a][0] + t_re, acc[a][1] + t_im)
                    if b == min(nb, a + lag_blocks + 1) - 1:
                        yh_ref[a, r_re, ls] = acc[a][0].astype(yh_ref.dtype)
                        yh_ref[a, r_im, ls] = acc[a][1].astype(yh_ref.dtype)
                        acc[a] = None


def _dw_spectra_kernel(e_ref, w_ref, o_ref):
    for c in range(e_ref.shape[0]):
        o_ref[c] = jnp.dot(e_ref[c], w_ref[...], preferred_element_type=F32)


def _dw_spectra(etab, wdw, *, td):
    nc, rows, taps = etab.shape
    d = wdw.shape[1]
    return pl.pallas_call(
        _dw_spectra_kernel,
        grid=(d // td,),
        in_specs=[_resident(etab.shape, lambda j: (0, 0, 0)), pl.BlockSpec((taps, td), lambda j: (0, j))],
        out_specs=pl.BlockSpec((nc, rows, td), lambda j: (0, 0, j)),
        out_shape=jax.ShapeDtypeStruct((nc, rows, d), F32),
        compiler_params=pltpu.CompilerParams(dimension_semantics=("arbitrary",)),
        name="dw_spectra",
    )(etab, wdw)


def _conf_in_kernel(x_ref, g_ref, wa_ref, wb_ref, ba_ref, bb_ref, gs_ref, bdw_ref, fwd_ref, inv_ref,
                    y_ref, xn_ref, u_ref, ph_ref, yh_ref, *, nb, blk, nt):
    t = pl.program_id(0)
    n_chunks = pl.num_programs(0) - 1

    @pl.when(t == 0)
    def _():
        ph_ref[...] = jnp.zeros(ph_ref.shape, ph_ref.dtype)

    @pl.when(jnp.logical_and(t % nt == 0, t < n_chunks))
    def _():
        xn_ref[...] = _rmsnorm_bf16(x_ref[0], g_ref[...])

    lag = 1

    def forward(b):
        ph_ref[b] = jnp.dot(fwd_ref[...], u_ref[b * blk:(b + 1) * blk, :], preferred_element_type=F32)

    for c in range(nb):
        rows = slice(c * blk, (c + 1) * blk)
        a1 = jnp.dot(xn_ref[rows, :], wa_ref[...], preferred_element_type=F32) + ba_ref[...]
        a2 = jnp.dot(xn_ref[rows, :], wb_ref[...], preferred_element_type=F32) + bb_ref[...]
        u_ref[rows, :] = (a1 * jax.nn.sigmoid(a2)).astype(BF16)
        _spectral_mix(gs_ref, ph_ref, yh_ref, nb=nb, blk=blk, lag_blocks=lag, out_blocks=(c,))
        y = jnp.dot(inv_ref[...], yh_ref[c], preferred_element_type=F32)
        y_ref[0, rows, :] = y + bdw_ref[...]
        if c - lag - 1 >= 0:
            forward(c - lag - 1)
    for b in range(max(nb - lag - 1, 0), nb):
        forward(b)


def _conf_in(x, g, w, b, gs, bdw, fwd, inv, *, tn, blk):
    bsz, seq, d = x.shape
    nt = d // tn
    nb = seq // blk
    n_chunks = bsz * nt

    def front(t):
        tf = jnp.minimum(t, n_chunks - 1)
        return tf // nt, tf % nt

    def back(t):
        tb = jnp.maximum(t - 1, 0)
        return tb // nt, tb % nt

    return pl.pallas_call(
        functools.partial(_conf_in_kernel, nb=nb, blk=blk, nt=nt),
        grid=(n_chunks + 1,),
        in_specs=[
            _resident((1, seq, d), lambda t: (front(t)[0], 0, 0)),
            _resident((1, d), lambda t: (0, 0)),
            pl.BlockSpec((d, tn), lambda t: (0, front(t)[1])),
            pl.BlockSpec((d, tn), lambda t: (0, nt + front(t)[1])),
            pl.BlockSpec((1, tn), lambda t: (0, front(t)[1])),
            pl.BlockSpec((1, tn), lambda t: (0, nt + front(t)[1])),
            pl.BlockSpec((gs.shape[0], 2 * blk, tn), lambda t: (0, 0, back(t)[1])),
            pl.BlockSpec((1, tn), lambda t: (0, back(t)[1])),
            _resident(fwd.shape, lambda t: (0, 0)),
            _resident(inv.shape, lambda t: (0, 0)),
        ],
        out_specs=pl.BlockSpec((1, seq, tn), lambda t: (back(t)[0], 0, back(t)[1])),
        out_shape=jax.ShapeDtypeStruct((bsz, seq, d), F32),
        scratch_shapes=[pltpu.VMEM((seq, d), BF16), pltpu.VMEM((seq, tn), BF16),
                        pltpu.VMEM((nb, 2 * blk, tn), F32), pltpu.VMEM((nb, 2 * blk, tn), BF16)],
        compiler_params=pltpu.CompilerParams(
            dimension_semantics=("arbitrary",),
            vmem_limit_bytes=_vmem_limit(seq * d * 6 + 8 * d * tn + 16 * seq * tn * 4 + (8 << 20))),
        name="conf_in",
    )(x, g, w, w, b, b, gs, bdw, fwd, inv)


def _conf_out_kernel(y_ref, x_ref, lg_ref, lb_ref, w2_ref, b2_ref, o_ref):
    y = y_ref[...]
    mu = jnp.mean(y, axis=-1, keepdims=True)
    yc = y - mu
    var = jnp.mean(yc * yc, axis=-1, keepdims=True)
    z = yc * lax.rsqrt(var + LN_EPS) * lg_ref[...] + lb_ref[...]
    act = (z * jax.nn.sigmoid(z)).astype(BF16)
    o_ref[...] = x_ref[...] + jnp.dot(act, w2_ref[...], preferred_element_type=F32) + b2_ref[...]


def _conf_out(y, x, lg, lb, w2, b2, *, tm):
    m, d = y.shape
    return pl.pallas_call(
        _conf_out_kernel,
        grid=(m // tm,),
        in_specs=[
            pl.BlockSpec((tm, d), lambda i: (i, 0)),
            pl.BlockSpec((tm, d), lambda i: (i, 0)),
            _resident((1, d), lambda i: (0, 0)),
            _resident((1, d), lambda i: (0, 0)),
            _resident((d, d), lambda i: (0, 0)),
            _resident((1, d), lambda i: (0, 0)),
        ],
        out_specs=pl.BlockSpec((tm, d), lambda i: (i, 0)),
        out_shape=jax.ShapeDtypeStruct((m, d), F32),
        compiler_params=pltpu.CompilerParams(
            dimension_semantics=("arbitrary",),
            vmem_limit_bytes=_vmem_limit(d * d * 2 + 12 * tm * d * 4 + (4 << 20))),
        name="conf_out",
    )(y, x, lg, lb, w2, b2)


def _ffn_kernel(x_ref, g_ref, wg_ref, wu_ref, wd_ref, gn_ref, o_ref, *rest, final_norm):
    xn_ref = rest[-1]
    j = pl.program_id(1)

    @pl.when(j == 0)
    def _():
        x = x_ref[...]
        xn_ref[...] = _rmsnorm_bf16(x, g_ref[...])
        o_ref[...] = x

    xn = xn_ref[...]
    gate = jnp.dot(xn, wg_ref[...], preferred_element_type=F32)
    up = jnp.dot(xn, wu_ref[...], preferred_element_type=F32)
    act = (gate * jax.nn.sigmoid(gate) * up).astype(BF16)
    o_ref[...] += jnp.dot(act, wd_ref[...], preferred_element_type=F32)

    @pl.when(j == pl.num_programs(1) - 1)
    def _():
        h = o_ref[...]
        ms = jnp.mean(h * h, axis=-1, keepdims=True)
        hn = h * lax.rsqrt(ms + NORM_EPS) * gn_ref[...]
        if final_norm:
            o_ref[...] = hn
        else:
            rest[0][...] = hn.astype(BF16)


def _ffn(x, g, wg, wu, wd, gn, *, layer, tm, tf, final_norm):
    m, d = x.shape
    f = wg.shape[-1]
    row_spec = pl.BlockSpec((tm, d), lambda i, j: (i, 0))
    out_specs, out_shape = row_spec, jax.ShapeDtypeStruct((m, d), F32)
    if not final_norm:
        out_specs = [row_spec, row_spec]
        out_shape = [out_shape, jax.ShapeDtypeStruct((m, d), BF16)]
    return pl.pallas_call(
        functools.partial(_ffn_kernel, final_norm=final_norm),
        grid=(m // tm, f // tf),
        in_specs=[
            row_spec,
            _resident((1, d), lambda i, j: (0, 0)),
            pl.BlockSpec((None, d, tf), lambda i, j: (layer, 0, j)),
            pl.BlockSpec((None, d, tf), lambda i, j: (layer, 0, j)),
            pl.BlockSpec((None, tf, d), lambda i, j: (layer, j, 0)),
            _resident((1, d), lambda i, j: (0, 0)),
        ],
        out_specs=out_specs,
        out_shape=out_shape,
        scratch_shapes=[pltpu.VMEM((tm, d), BF16)],
        compiler_params=pltpu.CompilerParams(
            dimension_semantics=("arbitrary", "arbitrary"),
            vmem_limit_bytes=_vmem_limit(5 * tm * d * 4 + 3 * tm * d * 2 + 12 * d * tf + 5 * tm * tf * 4 + (4 << 20))),
        name="ffn_final" if final_norm else "ffn",
    )(x, g, wg, wu, wd, gn)


def _hy_mix_kernel(xn_ref, w0_ref, w1_ref, w2_ref, bi0_ref, bi1_ref, bi2_ref,
                   ws0_ref, ws1_ref, ws2_ref, bs0_ref, bs1_ref, bs2_ref,
                   g_ref, fwd_ref, inv_ref, o_ref, z_ref, p_ref, x0_ref, ph_ref, yh_ref, *, nb, blk, rc):
    seq = xn_ref.shape[1]

    @pl.when(pl.program_id(0) == 0)
    def _():
        ph_ref[...] = jnp.zeros(ph_ref.shape, ph_ref.dtype)
        x0_ref[...] = jnp.zeros(x0_ref.shape, x0_ref.dtype)

    n_rc = seq // rc
    rc_per_blk = blk // rc
    n_slabs = blk // V7X_BF16_SUBLANES
    mix_chunks = n_rc // 2
    assert n_slabs % mix_chunks == 0 and n_rc - mix_chunks >= nb and blk % rc == 0
    slabs_per_chunk = n_slabs // mix_chunks
    halo = V7X_SUBLANES
    cur = pl.program_id(0) % 2
    branches = ((w0_ref, bi0_ref, ws0_ref, bs0_ref), (w1_ref, bi1_ref, ws1_ref, bs1_ref),
                (w2_ref, bi2_ref, ws2_ref, bs2_ref))

    def short_conv(k):
        lo, hi = max(k * rc - halo, 0), min((k + 1) * rc + halo, seq)
        n = hi - lo
        off = k * rc - lo
        row = lax.broadcasted_iota(jnp.int32, (n, 1), 0)
        outs = []
        for br, (_, _, ws_ref, bs_ref) in enumerate(branches):
            z = z_ref[br, lo:hi, :]
            zp = pltpu.roll(z, 1, axis=0)
            zn = pltpu.roll(z, n - 1, axis=0)
            if lo == 0:
                zp = jnp.where(row == 0, 0.0, zp)
            if hi == seq:
                zn = jnp.where(row == n - 1, 0.0, zn)
            y = ws_ref[0:1, :] * zp + ws_ref[1:2, :] * z + ws_ref[2:3, :] * zn + bs_ref[...]
            outs.append(y[off:off + rc, :])
        x0_ref[cur, pl.ds(k * rc, rc), :] = outs[0]
        p_ref[k * rc:(k + 1) * rc, :] = (outs[2] * outs[1]).astype(BF16)

    fwd_chunk = [max(rc_per_blk * (b + 1), mix_chunks + b) for b in range(nb)]

    def forward(b):
        ph_ref[b] = jnp.dot(fwd_ref[...], p_ref[b * blk:(b + 1) * blk, :], preferred_element_type=F32)

    for c in range(n_rc):
        rows = slice(c * rc, (c + 1) * rc)
        for br, (w_ref, bi_ref, _, _) in enumerate(branches):
            z_ref[br, rows, :] = jnp.dot(xn_ref[0, rows, :], w_ref[...], preferred_element_type=F32) + bi_ref[...]
        if c < mix_chunks:
            _spectral_mix(g_ref, ph_ref, yh_ref, nb=nb, blk=blk, lag_blocks=nb - 1,
                          slabs=range(c * slabs_per_chunk, (c + 1) * slabs_per_chunk))
        elif c - mix_chunks < nb:
            a = c - mix_chunks
            y = jnp.dot(inv_ref[...], yh_ref[a], preferred_element_type=F32)
            o_ref[0, a * blk:(a + 1) * blk, :] = (y * x0_ref[1 - cur, pl.ds(a * blk, blk), :]).astype(o_ref.dtype)
        if c >= 1:
            short_conv(c - 1)
        for b in range(nb):
            if fwd_chunk[b] == c:
                forward(b)
    short_conv(n_rc - 1)
    for b in range(nb):
        if fwd_chunk[b] >= n_rc:
            forward(b)


def _hy_mix(xn, w_in, b_in, w_short, b_short, gspec, fwd, inv, *, tn, nb, blk, rc):
    bsz, seq, d = xn.shape
    nt = d // tn
    n_items = nt * bsz

    def front(t):
        tf = jnp.minimum(t, n_items - 1)
        return tf // bsz, tf % bsz

    def back(t):
        tb = jnp.maximum(t - 1, 0)
        return tb // bsz, tb % bsz

    def col(k):
        return lambda t: (0, k * nt + front(t)[0])

    w_specs = [_resident((d, tn), col(k)) for k in range(3)]
    bi_specs = [pl.BlockSpec((1, tn), col(k)) for k in range(3)]
    ws_specs = [pl.BlockSpec((w_short.shape[0], tn), col(k)) for k in range(3)]
    bs_specs = [pl.BlockSpec((1, tn), col(k)) for k in range(3)]
    return pl.pallas_call(
        functools.partial(_hy_mix_kernel, nb=nb, blk=blk, rc=rc),
        grid=(n_items + 1,),
        in_specs=[pl.BlockSpec((1, seq, d), lambda t: (front(t)[1], 0, 0))]
        + w_specs + bi_specs + ws_specs + bs_specs
        + [_resident((2 * nb - 1, 2 * blk, tn), lambda t: (0, 0, back(t)[0])),
           _resident(fwd.shape, lambda t: (0, 0)),
           _resident(inv.shape, lambda t: (0, 0))],
        out_specs=pl.BlockSpec((1, seq, tn), lambda t: (back(t)[1], 0, back(t)[0])),
        out_shape=jax.ShapeDtypeStruct((bsz, seq, d), BF16),
        scratch_shapes=[pltpu.VMEM((3, seq, tn), F32), pltpu.VMEM((seq, tn), BF16), pltpu.VMEM((2, seq, tn), F32),
                        pltpu.VMEM((nb, 2 * blk, tn), F32), pltpu.VMEM((nb, 2 * blk, tn), BF16)],
        compiler_params=pltpu.CompilerParams(
            dimension_semantics=("arbitrary",),
            vmem_limit_bytes=_vmem_limit(V7X_VMEM_BYTES)),
        name="hy_mix",
    )(xn, w_in, w_in, w_in, b_in, b_in, b_in, w_short, w_short, w_short, b_short, b_short, b_short,
      gspec, fwd, inv)


def _filter_kernel(z_ref, w1_ref, b1_ref, f1_ref, w2_ref, b2_ref, f2_ref, w3_ref, b3_ref, f3_ref,
                   w4f_ref, w4b_ref, df_ref, skip_ref, fwd_ref, eneg_ref, g_ref, hf_ref, *, nb, blk):
    seq = z_ref.shape[0]

    @pl.when(pl.program_id(0) == 0)
    def _():
        hf = jnp.sin(f1_ref[...] * (jnp.dot(z_ref[...], w1_ref[...], preferred_element_type=F32) + b1_ref[...]))
        hf = jnp.sin(f2_ref[...] * (jnp.dot(hf, w2_ref[...], preferred_element_type=F32) + b2_ref[...]))
        hf_ref[...] = jnp.sin(f3_ref[...] * (jnp.dot(hf, w3_ref[...], preferred_element_type=F32) + b3_ref[...]))

    hf = hf_ref[...]
    row = lax.broadcasted_iota(jnp.int32, (seq, 1), 0)
    t = row.astype(F32) * (1.0 / (seq - 1))
    decay = jnp.exp(-t * df_ref[...])
    kf = jnp.dot(hf, w4f_ref[...], preferred_element_type=F32) * decay
    kb = jnp.dot(hf, w4b_ref[...], preferred_element_type=F32) * decay
    kb = jnp.where(row == 0, 0.0, kb)

    fwd = fwd_ref[...]
    eneg = eneg_ref[...]

    def spectra(k):
        pos, neg = [], []
        for jb in range(nb):
            kj = k[jb * blk:(jb + 1) * blk, :].astype(BF16)
            pos.append(jnp.dot(fwd, kj, preferred_element_type=F32))
            neg.append(jnp.dot(eneg, kj, preferred_element_type=F32) if jb < nb - 1 else None)
        return pos, neg

    fpos, fneg = spectra(kf)
    bpos, bneg = spectra(kb)

    def store(idx, val, conj_add=None):
        re, im = val[:blk], val[blk:]
        if conj_add is not None:
            re, im = re + conj_add[:blk] + skip_ref[...], im - conj_add[blk:]
        g_ref[idx, 0:blk, :] = re
        g_ref[idx, blk:, :] = im

    store(nb - 1, fpos[0], conj_add=bpos[0])
    for c in range(1, nb):
        store(nb - 1 + c, fneg[c - 1] + fpos[c])
        gb = bneg[c - 1] + bpos[c]
        g_ref[nb - 1 - c, 0:blk, :] = gb[:blk]
        g_ref[nb - 1 - c, blk:, :] = -gb[blk:]


def _filter_spectra(z, w1, b1, f1, w2, b2, f2, w3, b3, f3, w4, deltas, skip, fwd, eneg, *, td, nb, blk):
    seq = z.shape[0]
    d = deltas.shape[1]
    nt = d // td
    order = w2.shape[0]
    full = lambda a: _resident(a.shape, lambda j: (0,) * a.ndim)
    return pl.pallas_call(
        functools.partial(_filter_kernel, nb=nb, blk=blk),
        grid=(nt,),
        in_specs=[full(z), full(w1), full(b1), full(f1), full(w2), full(b2), full(f2), full(w3), full(b3), full(f3),
                  pl.BlockSpec((order, td), lambda j: (0, j)),
                  pl.BlockSpec((order, td), lambda j: (0, nt + j)),
                  pl.BlockSpec((1, td), lambda j: (0, j)),
                  pl.BlockSpec((1, td), lambda j: (0, j)),
                  full(fwd), full(eneg)],
        out_specs=pl.BlockSpec((2 * nb - 1, 2 * blk, td), lambda j: (0, 0, j)),
        out_shape=jax.ShapeDtypeStruct((2 * nb - 1, 2 * blk, d), F32),
        scratch_shapes=[pltpu.VMEM((seq, order), F32)],
        compiler_params=pltpu.CompilerParams(
            dimension_semantics=("arbitrary",),
            vmem_limit_bytes=_vmem_limit(2 * (2 * nb - 1) * 2 * blk * td * 4 + (4 * nb) * 2 * blk * td * 4
                                         + 6 * seq * td * 4 + (8 << 20))),
        name="hy_filter",
    )(z, w1, b1, f1, w2, b2, f2, w3, b3, f3, w4, w4, deltas, skip, fwd, eneg)


def _proj_res_kernel(a_ref, x_ref, w_ref, b_ref, o_ref):
    o_ref[...] = x_ref[...] + jnp.dot(a_ref[...], w_ref[...], preferred_element_type=F32) + b_ref[...]


def _proj_res(a, x, w, b, *, tm):
    m, d = x.shape
    return pl.pallas_call(
        _proj_res_kernel,
        grid=(m // tm,),
        in_specs=[pl.BlockSpec((tm, d), lambda i: (i, 0)),
                  pl.BlockSpec((tm, d), lambda i: (i, 0)),
                  _resident((d, d), lambda i: (0, 0)),
                  _resident((1, d), lambda i: (0, 0))],
        out_specs=pl.BlockSpec((tm, d), lambda i: (i, 0)),
        out_shape=jax.ShapeDtypeStruct((m, d), F32),
        compiler_params=pltpu.CompilerParams(
            dimension_semantics=("arbitrary",),
            vmem_limit_bytes=_vmem_limit(d * d * 2 + 7 * tm * d * 4 + (4 << 20))),
        name="hy_out",
    )(a, x, w, b)


def _dft_angles(blk):
    f = (np.arange(blk, dtype=np.float64) + 0.5)[:, None]
    return f * (2.0 * np.pi / (2 * blk))


def _dft_tables(blk):
    w = _dft_angles(blk)
    q = np.arange(blk, dtype=np.float64)[None, :]
    ang = w * q
    fwd = np.concatenate([np.cos(ang), -np.sin(ang)], axis=0)
    ang_n = w * (q - blk)
    eneg = np.concatenate([np.cos(ang_n), -np.sin(ang_n)], axis=0)
    eneg[:, 0] = 0.0
    inv = np.concatenate([np.cos(ang).T, -np.sin(ang).T], axis=1) / blk
    return (jnp.asarray(fwd, dtype=BF16), jnp.asarray(eneg, dtype=BF16), jnp.asarray(inv, dtype=BF16))


def _dw_tables(blk, width, taps_pad):
    w = _dft_angles(blk)
    pad = (width - 1) // 2
    tab = np.zeros((3, 2 * blk, taps_pad), dtype=np.float64)
    for ci, c in enumerate((-1, 0, 1)):
        for j in range(width):
            e = pad - j - blk * c
            if abs(e) <= blk - 1:
                tab[ci, :blk, j] = np.cos(w[:, 0] * e)
                tab[ci, blk:, j] = -np.sin(w[:, 0] * e)
    return jnp.asarray(tab, dtype=F32)


def _hyena_positional_features(seq, pad_to):
    t = jnp.linspace(0.0, 1.0, seq, dtype=F32)[:, None]
    bands = jnp.linspace(1e-4, HYENA_N_BANDS - 1, HYENA_N_BANDS, dtype=F32)[None, :]
    wpos = (2.0 * math.pi) * jnp.arange(seq, dtype=F32)[:, None] / seq
    z = jnp.concatenate([t, jnp.cos(bands * wpos), -jnp.sin(bands * wpos)], axis=-1)
    return jnp.pad(z, ((0, 0), (0, pad_to - z.shape[1])))


def _hyena_decay_rates(d):
    max_decay = math.log(HYENA_DECAY_TARGET) / HYENA_FAST_DECAY_PCT
    min_decay = math.log(HYENA_DECAY_TARGET) / HYENA_SLOW_DECAY_PCT
    return jnp.abs(jnp.linspace(min_decay, max_decay, d, dtype=F32))[None, :]


def _pick(n, prefs):
    for p in prefs:
        if n % p == 0:
            return p
    return n


def kernel(x, norm_mix, norm_ffn, cv_w_pw1, cv_b_pw1, cv_w_dw, cv_b_dw, cv_ln_g, cv_ln_b, cv_w_pw2, cv_b_pw2, hy_w_in, hy_b_in, hy_w_short, hy_b_short, hy_f_w1, hy_f_b1, hy_f_freq1, hy_f_w2, hy_f_b2, hy_f_freq2, hy_f_w3, hy_f_b3, hy_f_freq3, hy_f_w4, hy_skip, hy_w_out, hy_b_out, ffn_w_gate, ffn_w_up, ffn_w_down, norm_final):
    bsz, seq, d = x.shape
    m = bsz * seq
    f = ffn_w_gate.shape[-1]
    depth = norm_mix.shape[0]
    assert depth == 2 and cv_w_pw1.shape[0] == 1 and hy_w_in.shape[0] == 1

    tm = _pick(m, (512, 256, 128))
    tf = _pick(f, (512, 256, 128))
    tn_col = _pick(d, (256, 128))
    td = _pick(d, (512, 256, 128))
    blk = _pick(seq, (CONV_BLOCK, 256, 128))
    nb = seq // blk
    dw_blk = _pick(seq, (DW_BLOCK, 128))
    width = cv_w_dw.shape[1]
    assert (width - 1) // 2 < dw_blk

    row = lambda a: a.reshape(1, -1).astype(F32)
    bf = lambda a: a.astype(BF16)

    h = x.reshape(m, d)

    taps_pad = -(-width // V7X_SUBLANES) * V7X_SUBLANES
    wdw = jnp.pad(cv_w_dw[0], ((0, taps_pad - width), (0, 0)))
    dw_fwd, _, dw_inv = _dft_tables(dw_blk)
    dw_spec = _dw_spectra(_dw_tables(dw_blk, width, taps_pad), wdw, td=td)
    y = _conf_in(x, row(norm_mix[0]), bf(cv_w_pw1[0]), row(cv_b_pw1[0]), dw_spec, row(cv_b_dw[0]),
                 dw_fwd, dw_inv, tn=tn_col, blk=dw_blk)
    h = _conf_out(y.reshape(m, d), h, row(cv_ln_g[0]), row(cv_ln_b[0]), bf(cv_w_pw2[0]), row(cv_b_pw2[0]), tm=tm)
    wg, wu, wd = bf(ffn_w_gate), bf(ffn_w_up), bf(ffn_w_down)
    h, hn = _ffn(h, row(norm_ffn[0]), wg, wu, wd, row(norm_mix[1]), layer=0, tm=tm, tf=tf, final_norm=False)

    fwd, eneg, inv = _dft_tables(blk)
    emb = hy_f_w1.shape[1]
    emb_pad = -(-emb // V7X_LANES) * V7X_LANES
    z = _hyena_positional_features(seq, emb_pad)
    w1 = jnp.pad(hy_f_w1[0], ((0, emb_pad - emb), (0, 0)))
    gspec = _filter_spectra(z, w1, row(hy_f_b1[0]), row(hy_f_freq1[0]), hy_f_w2[0], row(hy_f_b2[0]),
                            row(hy_f_freq2[0]), hy_f_w3[0], row(hy_f_b3[0]), row(hy_f_freq3[0]), hy_f_w4[0],
                            _hyena_decay_rates(d), row(hy_skip[0]), fwd, eneg, td=td, nb=nb, blk=blk)
    gated = _hy_mix(hn.reshape(bsz, seq, d), bf(hy_w_in[0]), row(hy_b_in[0]), hy_w_short[0], row(hy_b_short[0]),
                    gspec, fwd, inv, tn=tn_col, nb=nb, blk=blk, rc=blk // 2)
    h = _proj_res(gated.reshape(m, d), h, bf(hy_w_out[0]), row(hy_b_out[0]), tm=tm)
    h = _ffn(h, row(norm_ffn[1]), wg, wu, wd, row(norm_final), layer=1, tm=tm, tf=tf, final_norm=True)
    return h.reshape(bsz, seq, d)
```

```python
import functools
import math

import jax
import jax.numpy as jnp
import numpy as np
from jax import lax
from jax.experimental import pallas as pl
from jax.experimental.pallas import tpu as pltpu

NORM_EPS = 1e-6
LN_EPS = 1e-5
HYENA_N_BANDS = 16
HYENA_FAST_DECAY_PCT = 0.3
HYENA_SLOW_DECAY_PCT = 1.5
HYENA_DECAY_TARGET = 1e-2

V7X_LANES = 128
V7X_SUBLANES = 8
V7X_BF16_SUBLANES = 16
V7X_VMEM_BYTES = 64 * 1024 * 1024

DW_BLOCK = 256
CONV_BLOCK = 512

F32 = jnp.float32
BF16 = jnp.bfloat16


def _vmem_limit(nbytes):
    return int(min(max(nbytes, 32 * 1024 * 1024), V7X_VMEM_BYTES - 4 * 1024 * 1024))


def _resident(block_shape, index_map):
    return pl.BlockSpec(block_shape, index_map, pipeline_mode=pl.Buffered(1))


def _rmsnorm_bf16(x, g):
    ms = jnp.mean(x * x, axis=-1, keepdims=True)
    return (x * lax.rsqrt(ms + NORM_EPS) * g).astype(BF16)


def _spectral_mix(g_ref, ph_ref, yh_ref, *, nb, blk, lag_blocks, slabs=None, out_blocks=None):
    lanes = yh_ref.shape[-1]
    slab = V7X_BF16_SUBLANES
    outs = range(nb) if out_blocks is None else out_blocks
    for s in (range(blk // slab) if slabs is None else slabs):
        r_re = slice(s * slab, (s + 1) * slab)
        r_im = slice(blk + s * slab, blk + (s + 1) * slab)
        for lc in range(lanes // V7X_LANES):
            ls = slice(lc * V7X_LANES, (lc + 1) * V7X_LANES)
            acc = [None] * nb
            for b in range(nb):
                targets = [a for a in range(max(0, b - lag_blocks), min(nb, b + lag_blocks + 1)) if a in outs]
                if not targets:
                    continue
                pre, pim = ph_ref[b, r_re, ls], ph_ref[b, r_im, ls]
                for a in targets:
                    gi = a - b + lag_blocks
                    gre, gim = g_ref[gi, r_re, ls], g_ref[gi, r_im, ls]
                    t_re = gre * pre - gim * pim
                    t_im = gre * pim + gim * pre
                    acc[a] = (t_re, t_im) if acc[a] is None else (acc[a][0] + t_re, acc[a][1] + t_im)
                    if b == min(nb, a + lag_blocks + 1) - 1:
                        yh_ref[a, r_re, ls] = acc[a][0].astype(yh_ref.dtype)
                        yh_ref[a, r_im, ls] = acc[a][1].astype(yh_ref.dtype)
                        acc[a] = None


def _dw_spectra_kernel(e_ref, w_ref, o_ref):
    for c in range(e_ref.shape[0]):
        o_ref[c] = jnp.dot(e_ref[c], w_ref[...], preferred_element_type=F32)


def _dw_spectra(etab, wdw, *, td):
    nc, rows, taps = etab.shape
    d = wdw.shape[1]
    return pl.pallas_call(
        _dw_spectra_kernel,
        grid=(d // td,),
        in_specs=[_resident(etab.shape, lambda j: (0, 0, 0)), pl.BlockSpec((taps, td), lambda j: (0, j))],
        out_specs=pl.BlockSpec((nc, rows, td), lambda j: (0, 0, j)),
        out_shape=jax.ShapeDtypeStruct((nc, rows, d), F32),
        compiler_params=pltpu.CompilerParams(dimension_semantics=("arbitrary",)),
        name="dw_spectra",
    )(etab, wdw)


def _conf_in_kernel(x_ref, g_ref, w_ref, b_ref, gs_ref, bdw_ref, fwd_ref, inv_ref,
                    y_ref, xn_ref, u_ref, ph_ref, yh_ref, *, nb, blk, nt):
    t = pl.program_id(0)
    n_chunks = pl.num_programs(0) - 1

    @pl.when(t == 0)
    def _():
        ph_ref[...] = jnp.zeros(ph_ref.shape, ph_ref.dtype)

    @pl.when(jnp.logical_and(t % nt == 0, t < n_chunks))
    def _():
        xn_ref[...] = _rmsnorm_bf16(x_ref[0], g_ref[...])

    lag = 1
    tn = y_ref.shape[-1]

    def forward(b):
        ph_ref[b] = jnp.dot(fwd_ref[...], u_ref[b * blk:(b + 1) * blk, :], preferred_element_type=F32)

    for c in range(nb):
        rows = slice(c * blk, (c + 1) * blk)
        a = jnp.dot(xn_ref[rows, :], w_ref[...], preferred_element_type=F32) + b_ref[...]
        u_ref[rows, :] = (a[:, :tn] * jax.nn.sigmoid(a[:, tn:])).astype(BF16)
        _spectral_mix(gs_ref, ph_ref, yh_ref, nb=nb, blk=blk, lag_blocks=lag, out_blocks=(c,))
        y = jnp.dot(inv_ref[...], yh_ref[c], preferred_element_type=F32)
        y_ref[0, rows, :] = y + bdw_ref[...]
        if c - lag - 1 >= 0:
            forward(c - lag - 1)
    for b in range(max(nb - lag - 1, 0), nb):
        forward(b)


def _conf_in(x, g, w, b, gs, bdw, fwd, inv, *, tn, blk):
    bsz, seq, d = x.shape
    nt = d // tn
    nb = seq // blk
    n_chunks = bsz * nt

    def front(t):
        tf = jnp.minimum(t, n_chunks - 1)
        return tf // nt, tf % nt

    def back(t):
        tb = jnp.maximum(t - 1, 0)
        return tb // nt, tb % nt

    def chunked(a):
        rows = a.shape[0]
        return a.reshape(rows, 2, nt, tn).transpose(2, 0, 1, 3).reshape(nt, rows, 2 * tn)

    return pl.pallas_call(
        functools.partial(_conf_in_kernel, nb=nb, blk=blk, nt=nt),
        grid=(n_chunks + 1,),
        in_specs=[
            _resident((1, seq, d), lambda t: (front(t)[0], 0, 0)),
            _resident((1, d), lambda t: (0, 0)),
            pl.BlockSpec((None, d, 2 * tn), lambda t: (front(t)[1], 0, 0)),
            pl.BlockSpec((None, 1, 2 * tn), lambda t: (front(t)[1], 0, 0)),
            pl.BlockSpec((gs.shape[0], 2 * blk, tn), lambda t: (0, 0, back(t)[1])),
            pl.BlockSpec((1, tn), lambda t: (0, back(t)[1])),
            _resident(fwd.shape, lambda t: (0, 0)),
            _resident(inv.shape, lambda t: (0, 0)),
        ],
        out_specs=pl.BlockSpec((1, seq, tn), lambda t: (back(t)[0], 0, back(t)[1])),
        out_shape=jax.ShapeDtypeStruct((bsz, seq, d), F32),
        scratch_shapes=[pltpu.VMEM((seq, d), BF16), pltpu.VMEM((seq, tn), BF16),
                        pltpu.VMEM((nb, 2 * blk, tn), F32), pltpu.VMEM((nb, 2 * blk, tn), BF16)],
        compiler_params=pltpu.CompilerParams(
            dimension_semantics=("arbitrary",),
            vmem_limit_bytes=_vmem_limit(seq * d * 6 + 8 * d * tn + 16 * seq * tn * 4 + (8 << 20))),
        name="conf_in",
    )(x, g, chunked(w), chunked(b), gs, bdw, fwd, inv)


def _conf_out_kernel(y_ref, x_ref, lg_ref, lb_ref, w2_ref, b2_ref, o_ref):
    y = y_ref[...]
    mu = jnp.mean(y, axis=-1, keepdims=True)
    yc = y - mu
    var = jnp.mean(yc * yc, axis=-1, keepdims=True)
    z = yc * lax.rsqrt(var + LN_EPS) * lg_ref[...] + lb_ref[...]
    act = (z * jax.nn.sigmoid(z)).astype(BF16)
    o_ref[...] = x_ref[...] + jnp.dot(act, w2_ref[...], preferred_element_type=F32) + b2_ref[...]


def _conf_out(y, x, lg, lb, w2, b2, *, tm):
    m, d = y.shape
    return pl.pallas_call(
        _conf_out_kernel,
        grid=(m // tm,),
        in_specs=[
            pl.BlockSpec((tm, d), lambda i: (i, 0)),
            pl.BlockSpec((tm, d), lambda i: (i, 0)),
            _resident((1, d), lambda i: (0, 0)),
            _resident((1, d), lambda i: (0, 0)),
            _resident((d, d), lambda i: (0, 0)),
            _resident((1, d), lambda i: (0, 0)),
        ],
        out_specs=pl.BlockSpec((tm, d), lambda i: (i, 0)),
        out_shape=jax.ShapeDtypeStruct((m, d), F32),
        compiler_params=pltpu.CompilerParams(
            dimension_semantics=("arbitrary",),
            vmem_limit_bytes=_vmem_limit(d * d * 2 + 12 * tm * d * 4 + (4 << 20))),
        name="conf_out",
    )(y, x, lg, lb, w2, b2)


def _ffn_kernel(x_ref, g_ref, wg_ref, wu_ref, wd_ref, gn_ref, o_ref, *rest, final_norm):
    xn_ref = rest[-1]
    j = pl.program_id(1)

    @pl.when(j == 0)
    def _():
        x = x_ref[...]
        xn_ref[...] = _rmsnorm_bf16(x, g_ref[...])
        o_ref[...] = x

    xn = xn_ref[...]
    gate = jnp.dot(xn, wg_ref[...], preferred_element_type=F32)
    up = jnp.dot(xn, wu_ref[...], preferred_element_type=F32)
    act = (gate * jax.nn.sigmoid(gate) * up).astype(BF16)
    o_ref[...] += jnp.dot(act, wd_ref[...], preferred_element_type=F32)

    @pl.when(j == pl.num_programs(1) - 1)
    def _():
        h = o_ref[...]
        ms = jnp.mean(h * h, axis=-1, keepdims=True)
        hn = h * lax.rsqrt(ms + NORM_EPS) * gn_ref[...]
        if final_norm:
            o_ref[...] = hn
        else:
            rest[0][...] = hn.astype(BF16)


def _ffn(x, g, wg, wu, wd, gn, *, layer, tm, tf, final_norm):
    m, d = x.shape
    f = wg.shape[-1]
    row_spec = pl.BlockSpec((tm, d), lambda i, j: (i, 0))
    out_specs, out_shape = row_spec, jax.ShapeDtypeStruct((m, d), F32)
    if not final_norm:
        out_specs = [row_spec, row_spec]
        out_shape = [out_shape, jax.ShapeDtypeStruct((m, d), BF16)]
    return pl.pallas_call(
        functools.partial(_ffn_kernel, final_norm=final_norm),
        grid=(m // tm, f // tf),
        in_specs=[
            row_spec,
            _resident((1, d), lambda i, j: (0, 0)),
            pl.BlockSpec((None, d, tf), lambda i, j: (layer, 0, j)),
            pl.BlockSpec((None, d, tf), lambda i, j: (layer, 0, j)),
            pl.BlockSpec((None, tf, d), lambda i, j: (layer, j, 0)),
            _resident((1, d), lambda i, j: (0, 0)),
        ],
        out_specs=out_specs,
        out_shape=out_shape,
        scratch_shapes=[pltpu.VMEM((tm, d), BF16)],
        compiler_params=pltpu.CompilerParams(
            dimension_semantics=("arbitrary", "arbitrary"),
            vmem_limit_bytes=_vmem_limit(5 * tm * d * 4 + 3 * tm * d * 2 + 12 * d * tf + 5 * tm * tf * 4 + (4 << 20))),
        name="ffn_final" if final_norm else "ffn",
    )(x, g, wg, wu, wd, gn)


def _hy_mix_kernel(xn_ref, w_ref, bi_ref, ws_ref, bs_ref,
                   g_ref, fwd_ref, inv_ref, o_ref, z_ref, p_ref, x0_ref, ph_ref, yh_ref, *, nb, blk, rc):
    seq = xn_ref.shape[1]

    @pl.when(pl.program_id(0) == 0)
    def _():
        ph_ref[...] = jnp.zeros(ph_ref.shape, ph_ref.dtype)
        x0_ref[...] = jnp.zeros(x0_ref.shape, x0_ref.dtype)

    n_rc = seq // rc
    rc_per_blk = blk // rc
    n_slabs = blk // V7X_BF16_SUBLANES
    mix_chunks = n_rc // 2
    assert n_slabs % mix_chunks == 0 and n_rc - mix_chunks >= nb and blk % rc == 0
    slabs_per_chunk = n_slabs // mix_chunks
    halo = V7X_SUBLANES
    cur = pl.program_id(0) % 2
    tn = o_ref.shape[-1]

    def short_conv(k):
        lo, hi = max(k * rc - halo, 0), min((k + 1) * rc + halo, seq)
        n = hi - lo
        off = k * rc - lo
        row = lax.broadcasted_iota(jnp.int32, (n, 1), 0)
        outs = []
        for br in range(3):
            cols = slice(br * tn, (br + 1) * tn)
            z = z_ref[lo:hi, cols]
            zp = pltpu.roll(z, 1, axis=0)
            zn = pltpu.roll(z, n - 1, axis=0)
            if lo == 0:
                zp = jnp.where(row == 0, 0.0, zp)
            if hi == seq:
                zn = jnp.where(row == n - 1, 0.0, zn)
            y = ws_ref[0:1, cols] * zp + ws_ref[1:2, cols] * z + ws_ref[2:3, cols] * zn + bs_ref[:, cols]
            outs.append(y[off:off + rc, :])
        x0_ref[cur, pl.ds(k * rc, rc), :] = outs[0]
        p_ref[k * rc:(k + 1) * rc, :] = (outs[2] * outs[1]).astype(BF16)

    fwd_chunk = [max(rc_per_blk * (b + 1), mix_chunks + b) for b in range(nb)]

    def forward(b):
        ph_ref[b] = jnp.dot(fwd_ref[...], p_ref[b * blk:(b + 1) * blk, :], preferred_element_type=F32)

    for c in range(n_rc):
        rows = slice(c * rc, (c + 1) * rc)
        z_ref[rows, :] = jnp.dot(xn_ref[0, rows, :], w_ref[...], preferred_element_type=F32) + bi_ref[...]
        if c < mix_chunks:
            _spectral_mix(g_ref, ph_ref, yh_ref, nb=nb, blk=blk, lag_blocks=nb - 1,
                          slabs=range(c * slabs_per_chunk, (c + 1) * slabs_per_chunk))
        elif c - mix_chunks < nb:
            a = c - mix_chunks
            y = jnp.dot(inv_ref[...], yh_ref[a], preferred_element_type=F32)
            o_ref[0, a * blk:(a + 1) * blk, :] = (y * x0_ref[1 - cur, pl.ds(a * blk, blk), :]).astype(o_ref.dtype)
        if c >= 1:
            short_conv(c - 1)
        for b in range(nb):
            if fwd_chunk[b] == c:
                forward(b)
    short_conv(n_rc - 1)
    for b in range(nb):
        if fwd_chunk[b] >= n_rc:
            forward(b)


def _hy_mix(xn, w_in, b_in, w_short, b_short, gspec, fwd, inv, *, tn, nb, blk, rc):
    bsz, seq, d = xn.shape
    nt = d // tn
    n_items = nt * bsz

    def front(t):
        tf = jnp.minimum(t, n_items - 1)
        return tf // bsz, tf % bsz

    def back(t):
        tb = jnp.maximum(t - 1, 0)
        return tb // bsz, tb % bsz

    def chunked(a):
        rows = a.shape[0]
        return a.reshape(rows, 3, nt, tn).transpose(2, 0, 1, 3).reshape(nt, rows, 3 * tn)

    def chunk_spec(rows, resident=False):
        mk = _resident if resident else pl.BlockSpec
        return mk((None, rows, 3 * tn), lambda t: (front(t)[0], 0, 0))

    return pl.pallas_call(
        functools.partial(_hy_mix_kernel, nb=nb, blk=blk, rc=rc),
        grid=(n_items + 1,),
        in_specs=[pl.BlockSpec((1, seq, d), lambda t: (front(t)[1], 0, 0)),
                  chunk_spec(d, resident=True), chunk_spec(1), chunk_spec(w_short.shape[0]), chunk_spec(1),
                  _resident((2 * nb - 1, 2 * blk, tn), lambda t: (0, 0, back(t)[0])),
                  _resident(fwd.shape, lambda t: (0, 0)),
                  _resident(inv.shape, lambda t: (0, 0))],
        out_specs=pl.BlockSpec((1, seq, tn), lambda t: (back(t)[1], 0, back(t)[0])),
        out_shape=jax.ShapeDtypeStruct((bsz, seq, d), BF16),
        scratch_shapes=[pltpu.VMEM((seq, 3 * tn), F32), pltpu.VMEM((seq, tn), BF16), pltpu.VMEM((2, seq, tn), F32),
                        pltpu.VMEM((nb, 2 * blk, tn), F32), pltpu.VMEM((nb, 2 * blk, tn), BF16)],
        compiler_params=pltpu.CompilerParams(
            dimension_semantics=("arbitrary",),
            vmem_limit_bytes=_vmem_limit(V7X_VMEM_BYTES)),
        name="hy_mix",
    )(xn, chunked(w_in), chunked(b_in), chunked(w_short), chunked(b_short), gspec, fwd, inv)


def _filter_kernel(z_ref, w1_ref, b1_ref, f1_ref, w2_ref, b2_ref, f2_ref, w3_ref, b3_ref, f3_ref,
                   w4f_ref, w4b_ref, df_ref, skip_ref, fwd_ref, eneg_ref, g_ref, hf_ref, *, nb, blk):
    seq = z_ref.shape[0]

    @pl.when(pl.program_id(0) == 0)
    def _():
        hf = jnp.sin(f1_ref[...] * (jnp.dot(z_ref[...], w1_ref[...], preferred_element_type=F32) + b1_ref[...]))
        hf = jnp.sin(f2_ref[...] * (jnp.dot(hf, w2_ref[...], preferred_element_type=F32) + b2_ref[...]))
        hf_ref[...] = jnp.sin(f3_ref[...] * (jnp.dot(hf, w3_ref[...], preferred_element_type=F32) + b3_ref[...]))

    hf = hf_ref[...]
    row = lax.broadcasted_iota(jnp.int32, (seq, 1), 0)
    t = row.astype(F32) * (1.0 / (seq - 1))
    decay = jnp.exp(-t * df_ref[...])
    kf = jnp.dot(hf, w4f_ref[...], preferred_element_type=F32) * decay
    kb = jnp.dot(hf, w4b_ref[...], preferred_element_type=F32) * decay
    kb = jnp.where(row == 0, 0.0, kb)

    fwd = fwd_ref[...]
    eneg = eneg_ref[...]

    def spectra(k):
        pos, neg = [], []
        for jb in range(nb):
            kj = k[jb * blk:(jb + 1) * blk, :].astype(BF16)
            pos.append(jnp.dot(fwd, kj, preferred_element_type=F32))
            neg.append(jnp.dot(eneg, kj, preferred_element_type=F32) if jb < nb - 1 else None)
        return pos, neg

    fpos, fneg = spectra(kf)
    bpos, bneg = spectra(kb)

    def store(idx, val, conj_add=None):
        re, im = val[:blk], val[blk:]
        if conj_add is not None:
            re, im = re + conj_add[:blk] + skip_ref[...], im - conj_add[blk:]
        g_ref[idx, 0:blk, :] = re
        g_ref[idx, blk:, :] = im

    store(nb - 1, fpos[0], conj_add=bpos[0])
    for c in range(1, nb):
        store(nb - 1 + c, fneg[c - 1] + fpos[c])
        gb = bneg[c - 1] + bpos[c]
        g_ref[nb - 1 - c, 0:blk, :] = gb[:blk]
        g_ref[nb - 1 - c, blk:, :] = -gb[blk:]


def _filter_spectra(z, w1, b1, f1, w2, b2, f2, w3, b3, f3, w4, deltas, skip, fwd, eneg, *, td, nb, blk):
    seq = z.shape[0]
    d = deltas.shape[1]
    nt = d // td
    order = w2.shape[0]
    full = lambda a: _resident(a.shape, lambda j: (0,) * a.ndim)
    return pl.pallas_call(
        functools.partial(_filter_kernel, nb=nb, blk=blk),
        grid=(nt,),
        in_specs=[full(z), full(w1), full(b1), full(f1), full(w2), full(b2), full(f2), full(w3), full(b3), full(f3),
                  pl.BlockSpec((order, td), lambda j: (0, j)),
                  pl.BlockSpec((order, td), lambda j: (0, nt + j)),
                  pl.BlockSpec((1, td), lambda j: (0, j)),
                  pl.BlockSpec((1, td), lambda j: (0, j)),
                  full(fwd), full(eneg)],
        out_specs=pl.BlockSpec((2 * nb - 1, 2 * blk, td), lambda j: (0, 0, j)),
        out_shape=jax.ShapeDtypeStruct((2 * nb - 1, 2 * blk, d), F32),
        scratch_shapes=[pltpu.VMEM((seq, order), F32)],
        compiler_params=pltpu.CompilerParams(
            dimension_semantics=("arbitrary",),
            vmem_limit_bytes=_vmem_limit(2 * (2 * nb - 1) * 2 * blk * td * 4 + (4 * nb) * 2 * blk * td * 4
                                         + 6 * seq * td * 4 + (8 << 20))),
        name="hy_filter",
    )(z, w1, b1, f1, w2, b2, f2, w3, b3, f3, w4, w4, deltas, skip, fwd, eneg)


def _proj_res_kernel(a_ref, x_ref, w_ref, b_ref, o_ref):
    o_ref[...] = x_ref[...] + jnp.dot(a_ref[...], w_ref[...], preferred_element_type=F32) + b_ref[...]


def _proj_res(a, x, w, b, *, tm):
    m, d = x.shape
    return pl.pallas_call(
        _proj_res_kernel,
        grid=(m // tm,),
        in_specs=[pl.BlockSpec((tm, d), lambda i: (i, 0)),
                  pl.BlockSpec((tm, d), lambda i: (i, 0)),
                  _resident((d, d), lambda i: (0, 0)),
                  _resident((1, d), lambda i: (0, 0))],
        out_specs=pl.BlockSpec((tm, d), lambda i: (i, 0)),
        out_shape=jax.ShapeDtypeStruct((m, d), F32),
        compiler_params=pltpu.CompilerParams(
            dimension_semantics=("arbitrary",),
            vmem_limit_bytes=_vmem_limit(d * d * 2 + 7 * tm * d * 4 + (4 << 20))),
        name="hy_out",
    )(a, x, w, b)


def _dft_angles(blk):
    f = (np.arange(blk, dtype=np.float64) + 0.5)[:, None]
    return f * (2.0 * np.pi / (2 * blk))


def _dft_tables(blk):
    w = _dft_angles(blk)
    q = np.arange(blk, dtype=np.float64)[None, :]
    ang = w * q
    fwd = np.concatenate([np.cos(ang), -np.sin(ang)], axis=0)
    ang_n = w * (q - blk)
    eneg = np.concatenate([np.cos(ang_n), -np.sin(ang_n)], axis=0)
    eneg[:, 0] = 0.0
    inv = np.concatenate([np.cos(ang).T, -np.sin(ang).T], axis=1) / blk
    return (jnp.asarray(fwd, dtype=BF16), jnp.asarray(eneg, dtype=BF16), jnp.asarray(inv, dtype=BF16))


def _dw_tables(blk, width, taps_pad):
    w = _dft_angles(blk)
    pad = (width - 1) // 2
    tab = np.zeros((3, 2 * blk, taps_pad), dtype=np.float64)
    for ci, c in enumerate((-1, 0, 1)):
        for j in range(width):
            e = pad - j - blk * c
            if abs(e) <= blk - 1:
                tab[ci, :blk, j] = np.cos(w[:, 0] * e)
                tab[ci, blk:, j] = -np.sin(w[:, 0] * e)
    return jnp.asarray(tab, dtype=F32)


def _hyena_positional_features(seq, pad_to):
    t = jnp.linspace(0.0, 1.0, seq, dtype=F32)[:, None]
    bands = jnp.linspace(1e-4, HYENA_N_BANDS - 1, HYENA_N_BANDS, dtype=F32)[None, :]
    wpos = (2.0 * math.pi) * jnp.arange(seq, dtype=F32)[:, None] / seq
    z = jnp.concatenate([t, jnp.cos(bands * wpos), -jnp.sin(bands * wpos)], axis=-1)
    return jnp.pad(z, ((0, 0), (0, pad_to - z.shape[1])))


def _hyena_decay_rates(d):
    max_decay = math.log(HYENA_DECAY_TARGET) / HYENA_FAST_DECAY_PCT
    min_decay = math.log(HYENA_DECAY_TARGET) / HYENA_SLOW_DECAY_PCT
    return jnp.abs(jnp.linspace(min_decay, max_decay, d, dtype=F32))[None, :]


def _pick(n, prefs):
    for p in prefs:
        if n % p == 0:
            return p
    return n


def kernel(x, norm_mix, norm_ffn, cv_w_pw1, cv_b_pw1, cv_w_dw, cv_b_dw, cv_ln_g, cv_ln_b, cv_w_pw2, cv_b_pw2, hy_w_in, hy_b_in, hy_w_short, hy_b_short, hy_f_w1, hy_f_b1, hy_f_freq1, hy_f_w2, hy_f_b2, hy_f_freq2, hy_f_w3, hy_f_b3, hy_f_freq3, hy_f_w4, hy_skip, hy_w_out, hy_b_out, ffn_w_gate, ffn_w_up, ffn_w_down, norm_final):
    bsz, seq, d = x.shape
    m = bsz * seq
    f = ffn_w_gate.shape[-1]
    depth = norm_mix.shape[0]
    assert depth == 2 and cv_w_pw1.shape[0] == 1 and hy_w_in.shape[0] == 1

    tm = _pick(m, (512, 256, 128))
    tf = _pick(f, (512, 256, 128))
    tn_col = _pick(d, (256, 128))
    td = _pick(d, (512, 256, 128))
    blk = _pick(seq, (CONV_BLOCK, 256, 128))
    nb = seq // blk
    dw_blk = _pick(seq, (DW_BLOCK, 128))
    width = cv_w_dw.shape[1]
    assert (width - 1) // 2 < dw_blk

    row = lambda a: a.reshape(1, -1).astype(F32)
    bf = lambda a: a.astype(BF16)

    h = x.reshape(m, d)

    taps_pad = -(-width // V7X_SUBLANES) * V7X_SUBLANES
    wdw = jnp.pad(cv_w_dw[0], ((0, taps_pad - width), (0, 0)))
    dw_fwd, _, dw_inv = _dft_tables(dw_blk)
    dw_spec = _dw_spectra(_dw_tables(dw_blk, width, taps_pad), wdw, td=td)
    y = _conf_in(x, row(norm_mix[0]), bf(cv_w_pw1[0]), row(cv_b_pw1[0]), dw_spec, row(cv_b_dw[0]),
                 dw_fwd, dw_inv, tn=tn_col, blk=dw_blk)
    h = _conf_out(y.reshape(m, d), h, row(cv_ln_g[0]), row(cv_ln_b[0]), bf(cv_w_pw2[0]), row(cv_b_pw2[0]), tm=tm)
    wg, wu, wd = bf(ffn_w_gate), bf(ffn_w_up), bf(ffn_w_down)
    h, hn = _ffn(h, row(norm_ffn[0]), wg, wu, wd, row(norm_mix[1]), layer=0, tm=tm, tf=tf, final_norm=False)

    fwd, eneg, inv = _dft_tables(blk)
    emb = hy_f_w1.shape[1]
    emb_pad = -(-emb // V7X_LANES) * V7X_LANES
    z = _hyena_positional_features(seq, emb_pad)
    w1 = jnp.pad(hy_f_w1[0], ((0, emb_pad - emb), (0, 0)))
    gspec = _filter_spectra(z, w1, row(hy_f_b1[0]), row(hy_f_freq1[0]), hy_f_w2[0], row(hy_f_b2[0]),
                            row(hy_f_freq2[0]), hy_f_w3[0], row(hy_f_b3[0]), row(hy_f_freq3[0]), hy_f_w4[0],
                            _hyena_decay_rates(d), row(hy_skip[0]), fwd, eneg, td=td, nb=nb, blk=blk)
    gated = _hy_mix(hn.reshape(bsz, seq, d), bf(hy_w_in[0]), row(hy_b_in[0]), hy_w_short[0], row(hy_b_short[0]),
                    gspec, fwd, inv, tn=tn_col, nb=nb, blk=blk, rc=blk // 2)
    h = _proj_res(gated.reshape(m, d), h, bf(hy_w_out[0]), row(hy_b_out[0]), tm=tm)
    h = _ffn(h, row(norm_ffn[1]), wg, wu, wd, row(norm_final), layer=1, tm=tm, tf=tf, final_norm=True)
    return h.reshape(bsz, seq, d)
```

```python
import functools
import math

import jax
import jax.numpy as jnp
import numpy as np
from jax import lax
from jax.experimental import pallas as pl
from jax.experimental.pallas import tpu as pltpu

NORM_EPS = 1e-6
LN_EPS = 1e-5
HYENA_N_BANDS = 16
HYENA_FAST_DECAY_PCT = 0.3
HYENA_SLOW_DECAY_PCT = 1.5
HYENA_DECAY_TARGET = 1e-2

V7X_LANES = 128
V7X_SUBLANES = 8
V7X_BF16_SUBLANES = 16
V7X_VMEM_BYTES = 64 * 1024 * 1024

MIX_LANES = 2 * V7X_LANES
MIX_OUT_GROUP = 2
DW_BLOCK = 256
CONV_BLOCK = 512

F32 = jnp.float32
BF16 = jnp.bfloat16


def _vmem_limit(nbytes):
    return int(min(max(nbytes, 32 * 1024 * 1024), V7X_VMEM_BYTES - 4 * 1024 * 1024))


def _resident(block_shape, index_map):
    return pl.BlockSpec(block_shape, index_map, pipeline_mode=pl.Buffered(1))


def _rmsnorm_bf16(x, g):
    ms = jnp.mean(x * x, axis=-1, keepdims=True)
    return (x * lax.rsqrt(ms + NORM_EPS) * g).astype(BF16)


def _spectral_mix(g_ref, ph_ref, yh_ref, *, nb, blk, lag_blocks, slabs=None, out_blocks=None):
    lanes = yh_ref.shape[-1]
    slab = V7X_BF16_SUBLANES
    lane_w = min(lanes, MIX_LANES)
    outs = list(range(nb) if out_blocks is None else out_blocks)
    groups = [outs[i:i + MIX_OUT_GROUP] for i in range(0, len(outs), MIX_OUT_GROUP)]
    for s in (range(blk // slab) if slabs is None else slabs):
        r_re = slice(s * slab, (s + 1) * slab)
        r_im = slice(blk + s * slab, blk + (s + 1) * slab)
        for lc in range(lanes // lane_w):
            ls = slice(lc * lane_w, (lc + 1) * lane_w)
            for group in groups:
                acc = {}
                for b in range(nb):
                    targets = [a for a in group if abs(a - b) <= lag_blocks]
                    if not targets:
                        continue
                    pre, pim = ph_ref[b, r_re, ls], ph_ref[b, r_im, ls]
                    for a in targets:
                        gi = a - b + lag_blocks
                        gre, gim = g_ref[gi, r_re, ls], g_ref[gi, r_im, ls]
                        t_re = gre * pre - gim * pim
                        t_im = gre * pim + gim * pre
                        acc[a] = (t_re, t_im) if a not in acc else (acc[a][0] + t_re, acc[a][1] + t_im)
                        if b == min(nb, a + lag_blocks + 1) - 1:
                            yh_ref[a, r_re, ls] = acc[a][0].astype(yh_ref.dtype)
                            yh_ref[a, r_im, ls] = acc[a][1].astype(yh_ref.dtype)
                            del acc[a]


def _dw_spectra_kernel(e_ref, w_ref, o_ref):
    for c in range(e_ref.shape[0]):
        o_ref[c] = jnp.dot(e_ref[c], w_ref[...], preferred_element_type=F32)


def _dw_spectra(etab, wdw, *, td):
    nc, rows, taps = etab.shape
    d = wdw.shape[1]
    return pl.pallas_call(
        _dw_spectra_kernel,
        grid=(d // td,),
        in_specs=[_resident(etab.shape, lambda j: (0, 0, 0)), pl.BlockSpec((taps, td), lambda j: (0, j))],
        out_specs=pl.BlockSpec((nc, rows, td), lambda j: (0, 0, j)),
        out_shape=jax.ShapeDtypeStruct((nc, rows, d), F32),
        compiler_params=pltpu.CompilerParams(dimension_semantics=("arbitrary",)),
        name="dw_spectra",
    )(etab, wdw)


def _conf_in_kernel(x_ref, g_ref, w_ref, b_ref, gs_ref, bdw_ref, fwd_ref, inv_ref,
                    y_ref, xn_ref, u_ref, ph_ref, yh_ref, *, nb, blk, nt):
    t = pl.program_id(0)
    n_chunks = pl.num_programs(0) - 1

    @pl.when(t == 0)
    def _():
        ph_ref[...] = jnp.zeros(ph_ref.shape, ph_ref.dtype)

    @pl.when(jnp.logical_and(t % nt == 0, t < n_chunks))
    def _():
        xn_ref[...] = _rmsnorm_bf16(x_ref[0], g_ref[...])

    lag = 1
    tn = y_ref.shape[-1]

    def forward(b):
        ph_ref[b] = jnp.dot(fwd_ref[...], u_ref[b * blk:(b + 1) * blk, :], preferred_element_type=F32)

    for c in range(nb):
        rows = slice(c * blk, (c + 1) * blk)
        a = jnp.dot(xn_ref[rows, :], w_ref[...], preferred_element_type=F32) + b_ref[...]
        u_ref[rows, :] = (a[:, :tn] * jax.nn.sigmoid(a[:, tn:])).astype(BF16)
        _spectral_mix(gs_ref, ph_ref, yh_ref, nb=nb, blk=blk, lag_blocks=lag, out_blocks=(c,))
        y = jnp.dot(inv_ref[...], yh_ref[c], preferred_element_type=F32)
        y_ref[0, rows, :] = y + bdw_ref[...]
        if c - lag - 1 >= 0:
            forward(c - lag - 1)
    for b in range(max(nb - lag - 1, 0), nb):
        forward(b)


def _conf_in(x, g, w, b, gs, bdw, fwd, inv, *, tn, blk):
    bsz, seq, d = x.shape
    nt = d // tn
    nb = seq // blk
    n_chunks = bsz * nt

    def front(t):
        tf = jnp.minimum(t, n_chunks - 1)
        return tf // nt, tf % nt

    def back(t):
        tb = jnp.maximum(t - 1, 0)
        return tb // nt, tb % nt

    def chunked(a):
        rows = a.shape[0]
        return a.reshape(rows, 2, nt, tn).transpose(2, 0, 1, 3).reshape(nt, rows, 2 * tn)

    return pl.pallas_call(
        functools.partial(_conf_in_kernel, nb=nb, blk=blk, nt=nt),
        grid=(n_chunks + 1,),
        in_specs=[
            _resident((1, seq, d), lambda t: (front(t)[0], 0, 0)),
            _resident((1, d), lambda t: (0, 0)),
            pl.BlockSpec((None, d, 2 * tn), lambda t: (front(t)[1], 0, 0)),
            pl.BlockSpec((None, 1, 2 * tn), lambda t: (front(t)[1], 0, 0)),
            pl.BlockSpec((gs.shape[0], 2 * blk, tn), lambda t: (0, 0, back(t)[1])),
            pl.BlockSpec((1, tn), lambda t: (0, back(t)[1])),
            _resident(fwd.shape, lambda t: (0, 0)),
            _resident(inv.shape, lambda t: (0, 0)),
        ],
        out_specs=pl.BlockSpec((1, seq, tn), lambda t: (back(t)[0], 0, back(t)[1])),
        out_shape=jax.ShapeDtypeStruct((bsz, seq, d), F32),
        scratch_shapes=[pltpu.VMEM((seq, d), BF16), pltpu.VMEM((seq, tn), BF16),
                        pltpu.VMEM((nb, 2 * blk, tn), F32), pltpu.VMEM((nb, 2 * blk, tn), BF16)],
        compiler_params=pltpu.CompilerParams(
            dimension_semantics=("arbitrary",),
            vmem_limit_bytes=_vmem_limit(seq * d * 6 + 8 * d * tn + 16 * seq * tn * 4 + (8 << 20))),
        name="conf_in",
    )(x, g, chunked(w), chunked(b), gs, bdw, fwd, inv)


def _conf_out_kernel(y_ref, x_ref, lg_ref, lb_ref, w2_ref, b2_ref, o_ref):
    y = y_ref[...]
    mu = jnp.mean(y, axis=-1, keepdims=True)
    yc = y - mu
    var = jnp.mean(yc * yc, axis=-1, keepdims=True)
    z = yc * lax.rsqrt(var + LN_EPS) * lg_ref[...] + lb_ref[...]
    act = (z * jax.nn.sigmoid(z)).astype(BF16)
    o_ref[...] = x_ref[...] + jnp.dot(act, w2_ref[...], preferred_element_type=F32) + b2_ref[...]


def _conf_out(y, x, lg, lb, w2, b2, *, tm):
    m, d = y.shape
    return pl.pallas_call(
        _conf_out_kernel,
        grid=(m // tm,),
        in_specs=[
            pl.BlockSpec((tm, d), lambda i: (i, 0)),
            pl.BlockSpec((tm, d), lambda i: (i, 0)),
            _resident((1, d), lambda i: (0, 0)),
            _resident((1, d), lambda i: (0, 0)),
            _resident((d, d), lambda i: (0, 0)),
            _resident((1, d), lambda i: (0, 0)),
        ],
        out_specs=pl.BlockSpec((tm, d), lambda i: (i, 0)),
        out_shape=jax.ShapeDtypeStruct((m, d), F32),
        compiler_params=pltpu.CompilerParams(
            dimension_semantics=("arbitrary",),
            vmem_limit_bytes=_vmem_limit(d * d * 2 + 12 * tm * d * 4 + (4 << 20))),
        name="conf_out",
    )(y, x, lg, lb, w2, b2)


def _ffn_kernel(x_ref, g_ref, wg_ref, wu_ref, wd_ref, gn_ref, o_ref, *rest, final_norm):
    xn_ref = rest[-1]
    j = pl.program_id(1)

    @pl.when(j == 0)
    def _():
        x = x_ref[...]
        xn_ref[...] = _rmsnorm_bf16(x, g_ref[...])
        o_ref[...] = x

    xn = xn_ref[...]
    gate = jnp.dot(xn, wg_ref[...], preferred_element_type=F32)
    up = jnp.dot(xn, wu_ref[...], preferred_element_type=F32)
    act = (gate * jax.nn.sigmoid(gate) * up).astype(BF16)
    o_ref[...] += jnp.dot(act, wd_ref[...], preferred_element_type=F32)

    @pl.when(j == pl.num_programs(1) - 1)
    def _():
        h = o_ref[...]
        ms = jnp.mean(h * h, axis=-1, keepdims=True)
        hn = h * lax.rsqrt(ms + NORM_EPS) * gn_ref[...]
        if final_norm:
            o_ref[...] = hn
        else:
            rest[0][...] = hn.astype(BF16)


def _ffn(x, g, wg, wu, wd, gn, *, layer, tm, tf, final_norm):
    m, d = x.shape
    f = wg.shape[-1]
    row_spec = pl.BlockSpec((tm, d), lambda i, j: (i, 0))
    out_specs, out_shape = row_spec, jax.ShapeDtypeStruct((m, d), F32)
    if not final_norm:
        out_specs = [row_spec, row_spec]
        out_shape = [out_shape, jax.ShapeDtypeStruct((m, d), BF16)]
    return pl.pallas_call(
        functools.partial(_ffn_kernel, final_norm=final_norm),
        grid=(m // tm, f // tf),
        in_specs=[
            row_spec,
            _resident((1, d), lambda i, j: (0, 0)),
            pl.BlockSpec((None, d, tf), lambda i, j: (layer, 0, j)),
            pl.BlockSpec((None, d, tf), lambda i, j: (layer, 0, j)),
            pl.BlockSpec((None, tf, d), lambda i, j: (layer, j, 0)),
            _resident((1, d), lambda i, j: (0, 0)),
        ],
        out_specs=out_specs,
        out_shape=out_shape,
        scratch_shapes=[pltpu.VMEM((tm, d), BF16)],
        compiler_params=pltpu.CompilerParams(
            dimension_semantics=("arbitrary", "arbitrary"),
            vmem_limit_bytes=_vmem_limit(5 * tm * d * 4 + 3 * tm * d * 2 + 12 * d * tf + 5 * tm * tf * 4 + (4 << 20))),
        name="ffn_final" if final_norm else "ffn",
    )(x, g, wg, wu, wd, gn)


def _hy_mix_kernel(xn_ref, w_ref, bi_ref, ws_ref, bs_ref,
                   g_ref, fwd_ref, inv_ref, o_ref, z_ref, p_ref, x0_ref, ph_ref, yh_ref, *, nb, blk, rc):
    seq = xn_ref.shape[1]

    @pl.when(pl.program_id(0) == 0)
    def _():
        ph_ref[...] = jnp.zeros(ph_ref.shape, ph_ref.dtype)
        x0_ref[...] = jnp.zeros(x0_ref.shape, x0_ref.dtype)

    n_rc = seq // rc
    rc_per_blk = blk // rc
    n_slabs = blk // V7X_BF16_SUBLANES
    mix_chunks = n_rc // 2
    assert n_slabs % mix_chunks == 0 and n_rc - mix_chunks >= nb and blk % rc == 0
    slabs_per_chunk = n_slabs // mix_chunks
    halo = V7X_SUBLANES
    cur = pl.program_id(0) % 2
    tn = o_ref.shape[-1]

    def short_conv(k):
        lo, hi = max(k * rc - halo, 0), min((k + 1) * rc + halo, seq)
        n = hi - lo
        off = k * rc - lo
        row = lax.broadcasted_iota(jnp.int32, (n, 1), 0)
        outs = []
        for br in range(3):
            cols = slice(br * tn, (br + 1) * tn)
            z = z_ref[lo:hi, cols]
            zp = pltpu.roll(z, 1, axis=0)
            zn = pltpu.roll(z, n - 1, axis=0)
            if lo == 0:
                zp = jnp.where(row == 0, 0.0, zp)
            if hi == seq:
                zn = jnp.where(row == n - 1, 0.0, zn)
            y = ws_ref[0:1, cols] * zp + ws_ref[1:2, cols] * z + ws_ref[2:3, cols] * zn + bs_ref[:, cols]
            outs.append(y[off:off + rc, :])
        x0_ref[cur, pl.ds(k * rc, rc), :] = outs[0]
        p_ref[k * rc:(k + 1) * rc, :] = (outs[2] * outs[1]).astype(BF16)

    fwd_chunk = [max(rc_per_blk * (b + 1), mix_chunks + b) for b in range(nb)]

    def forward(b):
        ph_ref[b] = jnp.dot(fwd_ref[...], p_ref[b * blk:(b + 1) * blk, :], preferred_element_type=F32)

    for c in range(n_rc):
        rows = slice(c * rc, (c + 1) * rc)
        z_ref[rows, :] = jnp.dot(xn_ref[0, rows, :], w_ref[...], preferred_element_type=F32) + bi_ref[...]
        if c < mix_chunks:
            _spectral_mix(g_ref, ph_ref, yh_ref, nb=nb, blk=blk, lag_blocks=nb - 1,
                          slabs=range(c * slabs_per_chunk, (c + 1) * slabs_per_chunk))
        elif c - mix_chunks < nb:
            a = c - mix_chunks
            y = jnp.dot(inv_ref[...], yh_ref[a], preferred_element_type=F32)
            o_ref[0, a * blk:(a + 1) * blk, :] = (y * x0_ref[1 - cur, pl.ds(a * blk, blk), :]).astype(o_ref.dtype)
        if c >= 1:
            short_conv(c - 1)
        for b in range(nb):
            if fwd_chunk[b] == c:
                forward(b)
    short_conv(n_rc - 1)
    for b in range(nb):
        if fwd_chunk[b] >= n_rc:
            forward(b)


def _hy_mix(xn, w_in, b_in, w_short, b_short, gspec, fwd, inv, *, tn, nb, blk, rc):
    bsz, seq, d = xn.shape
    nt = d // tn
    n_items = nt * bsz

    def front(t):
        tf = jnp.minimum(t, n_items - 1)
        return tf // bsz, tf % bsz

    def back(t):
        tb = jnp.maximum(t - 1, 0)
        return tb // bsz, tb % bsz

    def chunked(a):
        rows = a.shape[0]
        return a.reshape(rows, 3, nt, tn).transpose(2, 0, 1, 3).reshape(nt, rows, 3 * tn)

    def chunk_spec(rows, resident=False):
        mk = _resident if resident else pl.BlockSpec
        return mk((None, rows, 3 * tn), lambda t: (front(t)[0], 0, 0))

    return pl.pallas_call(
        functools.partial(_hy_mix_kernel, nb=nb, blk=blk, rc=rc),
        grid=(n_items + 1,),
        in_specs=[pl.BlockSpec((1, seq, d), lambda t: (front(t)[1], 0, 0)),
                  chunk_spec(d, resident=True), chunk_spec(1), chunk_spec(w_short.shape[0]), chunk_spec(1),
                  _resident((2 * nb - 1, 2 * blk, tn), lambda t: (0, 0, back(t)[0])),
                  _resident(fwd.shape, lambda t: (0, 0)),
                  _resident(inv.shape, lambda t: (0, 0))],
        out_specs=pl.BlockSpec((1, seq, tn), lambda t: (back(t)[1], 0, back(t)[0])),
        out_shape=jax.ShapeDtypeStruct((bsz, seq, d), BF16),
        scratch_shapes=[pltpu.VMEM((seq, 3 * tn), F32), pltpu.VMEM((seq, tn), BF16), pltpu.VMEM((2, seq, tn), F32),
                        pltpu.VMEM((nb, 2 * blk, tn), F32), pltpu.VMEM((nb, 2 * blk, tn), BF16)],
        compiler_params=pltpu.CompilerParams(
            dimension_semantics=("arbitrary",),
            vmem_limit_bytes=_vmem_limit(V7X_VMEM_BYTES)),
        name="hy_mix",
    )(xn, chunked(w_in), chunked(b_in), chunked(w_short), chunked(b_short), gspec, fwd, inv)


def _filter_kernel(z_ref, w1_ref, b1_ref, f1_ref, w2_ref, b2_ref, f2_ref, w3_ref, b3_ref, f3_ref,
                   w4f_ref, w4b_ref, df_ref, skip_ref, fwd_ref, eneg_ref, g_ref, hf_ref, *, nb, blk):
    seq = z_ref.shape[0]

    @pl.when(pl.program_id(0) == 0)
    def _():
        hf = jnp.sin(f1_ref[...] * (jnp.dot(z_ref[...], w1_ref[...], preferred_element_type=F32) + b1_ref[...]))
        hf = jnp.sin(f2_ref[...] * (jnp.dot(hf, w2_ref[...], preferred_element_type=F32) + b2_ref[...]))
        hf_ref[...] = jnp.sin(f3_ref[...] * (jnp.dot(hf, w3_ref[...], preferred_element_type=F32) + b3_ref[...]))

    hf = hf_ref[...]
    row = lax.broadcasted_iota(jnp.int32, (seq, 1), 0)
    t = row.astype(F32) * (1.0 / (seq - 1))
    decay = jnp.exp(-t * df_ref[...])
    kf = jnp.dot(hf, w4f_ref[...], preferred_element_type=F32) * decay
    kb = jnp.dot(hf, w4b_ref[...], preferred_element_type=F32) * decay
    kb = jnp.where(row == 0, 0.0, kb)

    fwd = fwd_ref[...]
    eneg = eneg_ref[...]

    def spectra(k):
        pos, neg = [], []
        for jb in range(nb):
            kj = k[jb * blk:(jb + 1) * blk, :].astype(BF16)
            pos.append(jnp.dot(fwd, kj, preferred_element_type=F32))
            neg.append(jnp.dot(eneg, kj, preferred_element_type=F32) if jb < nb - 1 else None)
        return pos, neg

    fpos, fneg = spectra(kf)
    bpos, bneg = spectra(kb)

    def store(idx, val, conj_add=None):
        re, im = val[:blk], val[blk:]
        if conj_add is not None:
            re, im = re + conj_add[:blk] + skip_ref[...], im - conj_add[blk:]
        g_ref[idx, 0:blk, :] = re
        g_ref[idx, blk:, :] = im

    store(nb - 1, fpos[0], conj_add=bpos[0])
    for c in range(1, nb):
        store(nb - 1 + c, fneg[c - 1] + fpos[c])
        gb = bneg[c - 1] + bpos[c]
        g_ref[nb - 1 - c, 0:blk, :] = gb[:blk]
        g_ref[nb - 1 - c, blk:, :] = -gb[blk:]


def _filter_spectra(z, w1, b1, f1, w2, b2, f2, w3, b3, f3, w4, deltas, skip, fwd, eneg, *, td, nb, blk):
    seq = z.shape[0]
    d = deltas.shape[1]
    nt = d // td
    order = w2.shape[0]
    full = lambda a: _resident(a.shape, lambda j: (0,) * a.ndim)
    return pl.pallas_call(
        functools.partial(_filter_kernel, nb=nb, blk=blk),
        grid=(nt,),
        in_specs=[full(z), full(w1), full(b1), full(f1), full(w2), full(b2), full(f2), full(w3), full(b3), full(f3),
                  pl.BlockSpec((order, td), lambda j: (0, j)),
                  pl.BlockSpec((order, td), lambda j: (0, nt + j)),
                  pl.BlockSpec((1, td), lambda j: (0, j)),
                  pl.BlockSpec((1, td), lambda j: (0, j)),
                  full(fwd), full(eneg)],
        out_specs=pl.BlockSpec((2 * nb - 1, 2 * blk, td), lambda j: (0, 0, j)),
        out_shape=jax.ShapeDtypeStruct((2 * nb - 1, 2 * blk, d), F32),
        scratch_shapes=[pltpu.VMEM((seq, order), F32)],
        compiler_params=pltpu.CompilerParams(
            dimension_semantics=("arbitrary",),
            vmem_limit_bytes=_vmem_limit(2 * (2 * nb - 1) * 2 * blk * td * 4 + (4 * nb) * 2 * blk * td * 4
                                         + 6 * seq * td * 4 + (8 << 20))),
        name="hy_filter",
    )(z, w1, b1, f1, w2, b2, f2, w3, b3, f3, w4, w4, deltas, skip, fwd, eneg)


def _proj_res_kernel(a_ref, x_ref, w_ref, b_ref, o_ref):
    o_ref[...] = x_ref[...] + jnp.dot(a_ref[...], w_ref[...], preferred_element_type=F32) + b_ref[...]


def _proj_res(a, x, w, b, *, tm):
    m, d = x.shape
    return pl.pallas_call(
        _proj_res_kernel,
        grid=(m // tm,),
        in_specs=[pl.BlockSpec((tm, d), lambda i: (i, 0)),
                  pl.BlockSpec((tm, d), lambda i: (i, 0)),
                  _resident((d, d), lambda i: (0, 0)),
                  _resident((1, d), lambda i: (0, 0))],
        out_specs=pl.BlockSpec((tm, d), lambda i: (i, 0)),
        out_shape=jax.ShapeDtypeStruct((m, d), F32),
        compiler_params=pltpu.CompilerParams(
            dimension_semantics=("arbitrary",),
            vmem_limit_bytes=_vmem_limit(d * d * 2 + 7 * tm * d * 4 + (4 << 20))),
        name="hy_out",
    )(a, x, w, b)


def _dft_angles(blk):
    f = (np.arange(blk, dtype=np.float64) + 0.5)[:, None]
    return f * (2.0 * np.pi / (2 * blk))


def _dft_tables(blk):
    w = _dft_angles(blk)
    q = np.arange(blk, dtype=np.float64)[None, :]
    ang = w * q
    fwd = np.concatenate([np.cos(ang), -np.sin(ang)], axis=0)
    ang_n = w * (q - blk)
    eneg = np.concatenate([np.cos(ang_n), -np.sin(ang_n)], axis=0)
    eneg[:, 0] = 0.0
    inv = np.concatenate([np.cos(ang).T, -np.sin(ang).T], axis=1) / blk
    return (jnp.asarray(fwd, dtype=BF16), jnp.asarray(eneg, dtype=BF16), jnp.asarray(inv, dtype=BF16))


def _dw_tables(blk, width, taps_pad):
    w = _dft_angles(blk)
    pad = (width - 1) // 2
    tab = np.zeros((3, 2 * blk, taps_pad), dtype=np.float64)
    for ci, c in enumerate((-1, 0, 1)):
        for j in range(width):
            e = pad - j - blk * c
            if abs(e) <= blk - 1:
                tab[ci, :blk, j] = np.cos(w[:, 0] * e)
                tab[ci, blk:, j] = -np.sin(w[:, 0] * e)
    return jnp.asarray(tab, dtype=F32)


def _hyena_positional_features(seq, pad_to):
    t = jnp.linspace(0.0, 1.0, seq, dtype=F32)[:, None]
    bands = jnp.linspace(1e-4, HYENA_N_BANDS - 1, HYENA_N_BANDS, dtype=F32)[None, :]
    wpos = (2.0 * math.pi) * jnp.arange(seq, dtype=F32)[:, None] / seq
    z = jnp.concatenate([t, jnp.cos(bands * wpos), -jnp.sin(bands * wpos)], axis=-1)
    return jnp.pad(z, ((0, 0), (0, pad_to - z.shape[1])))


def _hyena_decay_rates(d):
    max_decay = math.log(HYENA_DECAY_TARGET) / HYENA_FAST_DECAY_PCT
    min_decay = math.log(HYENA_DECAY_TARGET) / HYENA_SLOW_DECAY_PCT
    return jnp.abs(jnp.linspace(min_decay, max_decay, d, dtype=F32))[None, :]


def _pick(n, prefs):
    for p in prefs:
        if n % p == 0:
            return p
    return n


def kernel(x, norm_mix, norm_ffn, cv_w_pw1, cv_b_pw1, cv_w_dw, cv_b_dw, cv_ln_g, cv_ln_b, cv_w_pw2, cv_b_pw2, hy_w_in, hy_b_in, hy_w_short, hy_b_short, hy_f_w1, hy_f_b1, hy_f_freq1, hy_f_w2, hy_f_b2, hy_f_freq2, hy_f_w3, hy_f_b3, hy_f_freq3, hy_f_w4, hy_skip, hy_w_out, hy_b_out, ffn_w_gate, ffn_w_up, ffn_w_down, norm_final):
    bsz, seq, d = x.shape
    m = bsz * seq
    f = ffn_w_gate.shape[-1]
    depth = norm_mix.shape[0]
    assert depth == 2 and cv_w_pw1.shape[0] == 1 and hy_w_in.shape[0] == 1

    tm = _pick(m, (512, 256, 128))
    tf = _pick(f, (512, 256, 128))
    tn_col = _pick(d, (256, 128))
    td = _pick(d, (512, 256, 128))
    blk = _pick(seq, (CONV_BLOCK, 256, 128))
    nb = seq // blk
    dw_blk = _pick(seq, (DW_BLOCK, 128))
    width = cv_w_dw.shape[1]
    assert (width - 1) // 2 < dw_blk

    row = lambda a: a.reshape(1, -1).astype(F32)
    bf = lambda a: a.astype(BF16)

    h = x.reshape(m, d)

    taps_pad = -(-width // V7X_SUBLANES) * V7X_SUBLANES
    wdw = jnp.pad(cv_w_dw[0], ((0, taps_pad - width), (0, 0)))
    dw_fwd, _, dw_inv = _dft_tables(dw_blk)
    dw_spec = _dw_spectra(_dw_tables(dw_blk, width, taps_pad), wdw, td=td)
    y = _conf_in(x, row(norm_mix[0]), bf(cv_w_pw1[0]), row(cv_b_pw1[0]), dw_spec, row(cv_b_dw[0]),
                 dw_fwd, dw_inv, tn=tn_col, blk=dw_blk)
    h = _conf_out(y.reshape(m, d), h, row(cv_ln_g[0]), row(cv_ln_b[0]), bf(cv_w_pw2[0]), row(cv_b_pw2[0]), tm=tm)
    wg, wu, wd = bf(ffn_w_gate), bf(ffn_w_up), bf(ffn_w_down)
    h, hn = _ffn(h, row(norm_ffn[0]), wg, wu, wd, row(norm_mix[1]), layer=0, tm=tm, tf=tf, final_norm=False)

    fwd, eneg, inv = _dft_tables(blk)
    emb = hy_f_w1.shape[1]
    emb_pad = -(-emb // V7X_LANES) * V7X_LANES
    z = _hyena_positional_features(seq, emb_pad)
    w1 = jnp.pad(hy_f_w1[0], ((0, emb_pad - emb), (0, 0)))
    gspec = _filter_spectra(z, w1, row(hy_f_b1[0]), row(hy_f_freq1[0]), hy_f_w2[0], row(hy_f_b2[0]),
                            row(hy_f_freq2[0]), hy_f_w3[0], row(hy_f_b3[0]), row(hy_f_freq3[0]), hy_f_w4[0],
                            _hyena_decay_rates(d), row(hy_skip[0]), fwd, eneg, td=td, nb=nb, blk=blk)
    gated = _hy_mix(hn.reshape(bsz, seq, d), bf(hy_w_in[0]), row(hy_b_in[0]), hy_w_short[0], row(hy_b_short[0]),
                    gspec, fwd, inv, tn=tn_col, nb=nb, blk=blk, rc=blk // 2)
    h = _proj_res(gated.reshape(m, d), h, bf(hy_w_out[0]), row(hy_b_out[0]), tm=tm)
    h = _ffn(h, row(norm_ffn[1]), wg, wu, wd, row(norm_final), layer=1, tm=tm, tf=tf, final_norm=True)
    return h.reshape(bsz, seq, d)
```

```python
import functools
import math

import jax
import jax.numpy as jnp
import numpy as np
from jax import lax
from jax.experimental import pallas as pl
from jax.experimental.pallas import tpu as pltpu

NORM_EPS = 1e-6
LN_EPS = 1e-5
HYENA_N_BANDS = 16
HYENA_FAST_DECAY_PCT = 0.3
HYENA_SLOW_DECAY_PCT = 1.5
HYENA_DECAY_TARGET = 1e-2

V7X_LANES = 128
V7X_SUBLANES = 8
V7X_BF16_SUBLANES = 16
V7X_VMEM_BYTES = 64 * 1024 * 1024

DW_BLOCK = 256
CONV_BLOCK = 512
FFN_NORM_ROWS = 64

F32 = jnp.float32
BF16 = jnp.bfloat16


def _vmem_limit(nbytes):
    return int(min(max(nbytes, 32 * 1024 * 1024), V7X_VMEM_BYTES - 4 * 1024 * 1024))


def _resident(block_shape, index_map):
    return pl.BlockSpec(block_shape, index_map, pipeline_mode=pl.Buffered(1))


def _rmsnorm_bf16(x, g):
    ms = jnp.mean(x * x, axis=-1, keepdims=True)
    return (x * lax.rsqrt(ms + NORM_EPS) * g).astype(BF16)


def _spectral_mix(g_ref, ph_ref, yh_ref, a, *, nb, blk, lag_blocks):
    lanes = yh_ref.shape[-1]
    slab = V7X_BF16_SUBLANES
    lane_w = min(lanes, 2 * V7X_LANES)
    for s in range(blk // slab):
        r_re = slice(s * slab, (s + 1) * slab)
        r_im = slice(blk + s * slab, blk + (s + 1) * slab)
        for lc in range(lanes // lane_w):
            ls = slice(lc * lane_w, (lc + 1) * lane_w)
            acc_re = acc_im = None
            for b in range(max(0, a - lag_blocks), min(nb, a + lag_blocks + 1)):
                gi = a - b + lag_blocks
                gre, gim = g_ref[gi, r_re, ls], g_ref[gi, r_im, ls]
                pre, pim = ph_ref[b, r_re, ls], ph_ref[b, r_im, ls]
                t_re = gre * pre - gim * pim
                t_im = gre * pim + gim * pre
                acc_re = t_re if acc_re is None else acc_re + t_re
                acc_im = t_im if acc_im is None else acc_im + t_im
            yh_ref[r_re, ls] = acc_re.astype(yh_ref.dtype)
            yh_ref[r_im, ls] = acc_im.astype(yh_ref.dtype)


def _dw_spectra_kernel(e_ref, w_ref, o_ref):
    for c in range(e_ref.shape[0]):
        o_ref[c] = jnp.dot(e_ref[c], w_ref[...], preferred_element_type=F32)


def _dw_spectra(etab, wdw, *, td):
    nc, rows, taps = etab.shape
    d = wdw.shape[1]
    return pl.pallas_call(
        _dw_spectra_kernel,
        grid=(d // td,),
        in_specs=[_resident(etab.shape, lambda j: (0, 0, 0)), pl.BlockSpec((taps, td), lambda j: (0, j))],
        out_specs=pl.BlockSpec((nc, rows, td), lambda j: (0, 0, j)),
        out_shape=jax.ShapeDtypeStruct((nc, rows, d), F32),
        compiler_params=pltpu.CompilerParams(dimension_semantics=("arbitrary",)),
        name="dw_spectra",
    )(etab, wdw)


def _conf_in_kernel(x_ref, g_ref, wa_ref, wb_ref, ba_ref, bb_ref, gs_ref, bdw_ref, fwd_ref, inv_ref,
                    y_ref, xn_ref, ph_ref, yh_ref, *, nb, blk):
    @pl.when(pl.program_id(1) == 0)
    def _():
        xn_ref[...] = _rmsnorm_bf16(x_ref[0], g_ref[...])

    xn = xn_ref[...]
    a1 = jnp.dot(xn, wa_ref[...], preferred_element_type=F32) + ba_ref[...]
    a2 = jnp.dot(xn, wb_ref[...], preferred_element_type=F32) + bb_ref[...]
    u = (a1 * jax.nn.sigmoid(a2)).astype(BF16)
    for b in range(nb):
        ph_ref[b] = jnp.dot(fwd_ref[...], u[b * blk:(b + 1) * blk, :], preferred_element_type=F32)
    for a in range(nb):
        _spectral_mix(gs_ref, ph_ref, yh_ref, a, nb=nb, blk=blk, lag_blocks=1)
        y = jnp.dot(inv_ref[...], yh_ref[...], preferred_element_type=F32)
        y_ref[0, a * blk:(a + 1) * blk, :] = y + bdw_ref[...]


def _conf_in(x, g, w, b, gs, bdw, fwd, inv, *, tn, blk):
    bsz, seq, d = x.shape
    nt = d // tn
    nb = seq // blk
    return pl.pallas_call(
        functools.partial(_conf_in_kernel, nb=nb, blk=blk),
        grid=(bsz, nt),
        in_specs=[
            _resident((1, seq, d), lambda i, j: (i, 0, 0)),
            _resident((1, d), lambda i, j: (0, 0)),
            pl.BlockSpec((d, tn), lambda i, j: (0, j)),
            pl.BlockSpec((d, tn), lambda i, j: (0, nt + j)),
            pl.BlockSpec((1, tn), lambda i, j: (0, j)),
            pl.BlockSpec((1, tn), lambda i, j: (0, nt + j)),
            pl.BlockSpec((gs.shape[0], 2 * blk, tn), lambda i, j: (0, 0, j)),
            pl.BlockSpec((1, tn), lambda i, j: (0, j)),
            _resident(fwd.shape, lambda i, j: (0, 0)),
            _resident(inv.shape, lambda i, j: (0, 0)),
        ],
        out_specs=pl.BlockSpec((1, seq, tn), lambda i, j: (i, 0, j)),
        out_shape=jax.ShapeDtypeStruct((bsz, seq, d), F32),
        scratch_shapes=[pltpu.VMEM((seq, d), BF16), pltpu.VMEM((nb, 2 * blk, tn), F32),
                        pltpu.VMEM((2 * blk, tn), BF16)],
        compiler_params=pltpu.CompilerParams(
            dimension_semantics=("arbitrary", "arbitrary"),
            vmem_limit_bytes=_vmem_limit(seq * d * 6 + 8 * d * tn + 14 * seq * tn * 4 + (6 << 20))),
        name="conf_in",
    )(x, g, w, w, b, b, gs, bdw, fwd, inv)


def _conf_out_kernel(y_ref, x_ref, lg_ref, lb_ref, w2_ref, b2_ref, o_ref):
    y = y_ref[...]
    mu = jnp.mean(y, axis=-1, keepdims=True)
    yc = y - mu
    var = jnp.mean(yc * yc, axis=-1, keepdims=True)
    z = yc * lax.rsqrt(var + LN_EPS) * lg_ref[...] + lb_ref[...]
    act = (z * jax.nn.sigmoid(z)).astype(BF16)
    o_ref[...] = x_ref[...] + jnp.dot(act, w2_ref[...], preferred_element_type=F32) + b2_ref[...]


def _conf_out(y, x, lg, lb, w2, b2, *, tm):
    m, d = y.shape
    return pl.pallas_call(
        _conf_out_kernel,
        grid=(m // tm,),
        in_specs=[
            pl.BlockSpec((tm, d), lambda i: (i, 0)),
            pl.BlockSpec((tm, d), lambda i: (i, 0)),
            _resident((1, d), lambda i: (0, 0)),
            _resident((1, d), lambda i: (0, 0)),
            _resident((d, d), lambda i: (0, 0)),
            _resident((1, d), lambda i: (0, 0)),
        ],
        out_specs=pl.BlockSpec((tm, d), lambda i: (i, 0)),
        out_shape=jax.ShapeDtypeStruct((m, d), F32),
        compiler_params=pltpu.CompilerParams(
            dimension_semantics=("arbitrary",),
            vmem_limit_bytes=_vmem_limit(d * d * 2 + 12 * tm * d * 4 + (4 << 20))),
        name="conf_out",
    )(y, x, lg, lb, w2, b2)


def _ffn_kernel(x_ref, xnext_ref, g_ref, wg_ref, wu_ref, wd_ref, gn_ref, o_ref, *rest, final_norm, norm_rows):
    xn_even, xn_odd = rest[-2:]
    i, j = pl.program_id(0), pl.program_id(1)
    n_j = pl.num_programs(1)
    n_slices = o_ref.shape[0] // norm_rows

    @pl.when(jnp.logical_and(i == 0, j == 0))
    def _():
        xn_even[...] = _rmsnorm_bf16(x_ref[...], g_ref[...])

    @pl.when(j == 0)
    def _():
        o_ref[...] = x_ref[...]

    def step(xn_ref, xn_next_ref):
        c = jnp.clip(j - (n_j - n_slices), 0, n_slices - 1)
        rows = pl.ds(pl.multiple_of(c * norm_rows, norm_rows), norm_rows)
        xn_next_ref[rows, :] = _rmsnorm_bf16(xnext_ref[rows, :], g_ref[...])
        xn = xn_ref[...]
        gate = jnp.dot(xn, wg_ref[...], preferred_element_type=F32)
        up = jnp.dot(xn, wu_ref[...], preferred_element_type=F32)
        act = (gate * jax.nn.sigmoid(gate) * up).astype(BF16)
        o_ref[...] += jnp.dot(act, wd_ref[...], preferred_element_type=F32)

    @pl.when(i % 2 == 0)
    def _():
        step(xn_even, xn_odd)

    @pl.when(i % 2 == 1)
    def _():
        step(xn_odd, xn_even)

    @pl.when(j == n_j - 1)
    def _():
        h = o_ref[...]
        ms = jnp.mean(h * h, axis=-1, keepdims=True)
        hn = h * lax.rsqrt(ms + NORM_EPS) * gn_ref[...]
        if final_norm:
            o_ref[...] = hn
        else:
            rest[0][...] = hn.astype(BF16)


def _ffn(x, g, wg, wu, wd, gn, *, layer, tm, tf, final_norm):
    m, d = x.shape
    f = wg.shape[-1]
    n_i = m // tm
    n_slices = 1
    while 2 * n_slices <= min(f // tf, tm // FFN_NORM_ROWS):
        n_slices *= 2
    norm_rows = tm // n_slices
    assert norm_rows % V7X_BF16_SUBLANES == 0
    row_spec = pl.BlockSpec((tm, d), lambda i, j: (i, 0))
    out_specs, out_shape = row_spec, jax.ShapeDtypeStruct((m, d), F32)
    if not final_norm:
        out_specs = [row_spec, row_spec]
        out_shape = [out_shape, jax.ShapeDtypeStruct((m, d), BF16)]
    return pl.pallas_call(
        functools.partial(_ffn_kernel, final_norm=final_norm, norm_rows=norm_rows),
        grid=(n_i, f // tf),
        in_specs=[
            row_spec,
            pl.BlockSpec((tm, d), lambda i, j: (jnp.minimum(i + 1, n_i - 1), 0)),
            _resident((1, d), lambda i, j: (0, 0)),
            pl.BlockSpec((None, d, tf), lambda i, j: (layer, 0, j)),
            pl.BlockSpec((None, d, tf), lambda i, j: (layer, 0, j)),
            pl.BlockSpec((None, tf, d), lambda i, j: (layer, j, 0)),
            _resident((1, d), lambda i, j: (0, 0)),
        ],
        out_specs=out_specs,
        out_shape=out_shape,
        scratch_shapes=[pltpu.VMEM((tm, d), BF16), pltpu.VMEM((tm, d), BF16)],
        compiler_params=pltpu.CompilerParams(
            dimension_semantics=("arbitrary", "arbitrary"),
            vmem_limit_bytes=_vmem_limit(7 * tm * d * 4 + 4 * tm * d * 2 + 12 * d * tf + 5 * tm * tf * 4 + (4 << 20))),
        name="ffn_final" if final_norm else "ffn",
    )(x, x, g, wg, wu, wd, gn)


def _hy_mix_kernel(xn_ref, w0_ref, w1_ref, w2_ref, bi0_ref, bi1_ref, bi2_ref,
                   ws0_ref, ws1_ref, ws2_ref, bs0_ref, bs1_ref, bs2_ref,
                   g_ref, skip_ref, fwd_ref, inv_ref, o_ref, p_ref, x0_ref, ph_ref, yh_ref, *, nb, blk):
    seq = xn_ref.shape[1]
    xn = xn_ref[0]
    row = lax.broadcasted_iota(jnp.int32, (seq, 1), 0)
    first = row == 0
    last = row == seq - 1

    def branch(w_ref, bi_ref, ws_ref, bs_ref):
        z = jnp.dot(xn, w_ref[...], preferred_element_type=F32) + bi_ref[...]
        zp = jnp.where(first, 0.0, pltpu.roll(z, 1, axis=0))
        zn = jnp.where(last, 0.0, pltpu.roll(z, seq - 1, axis=0))
        return ws_ref[0:1, :] * zp + ws_ref[1:2, :] * z + ws_ref[2:3, :] * zn + bs_ref[...]

    x1 = branch(w1_ref, bi1_ref, ws1_ref, bs1_ref)
    v = branch(w2_ref, bi2_ref, ws2_ref, bs2_ref)
    p_ref[...] = v * x1
    x0_ref[...] = branch(w0_ref, bi0_ref, ws0_ref, bs0_ref)

    for b in range(nb):
        pb = p_ref[b * blk:(b + 1) * blk, :].astype(BF16)
        ph_ref[b] = jnp.dot(fwd_ref[...], pb, preferred_element_type=F32)
    for a in range(nb):
        _spectral_mix(g_ref, ph_ref, yh_ref, a, nb=nb, blk=blk, lag_blocks=nb - 1)
        y = jnp.dot(inv_ref[...], yh_ref[...], preferred_element_type=F32)
        rows = slice(a * blk, (a + 1) * blk)
        y = y + p_ref[rows, :] * skip_ref[...]
        o_ref[0, rows, :] = (y * x0_ref[rows, :]).astype(o_ref.dtype)


def _hy_mix(xn, w_in, b_in, w_short, b_short, gspec, skip, fwd, inv, *, tn, nb, blk):
    b, seq, d = xn.shape
    nt = d // tn

    def col(k):
        return lambda j, i: (0, k * nt + j)

    w_specs = [_resident((d, tn), col(k)) for k in range(3)]
    bi_specs = [pl.BlockSpec((1, tn), col(k)) for k in range(3)]
    ws_specs = [pl.BlockSpec((w_short.shape[0], tn), col(k)) for k in range(3)]
    bs_specs = [pl.BlockSpec((1, tn), col(k)) for k in range(3)]
    return pl.pallas_call(
        functools.partial(_hy_mix_kernel, nb=nb, blk=blk),
        grid=(nt, b),
        in_specs=[pl.BlockSpec((1, seq, d), lambda j, i: (i, 0, 0))]
        + w_specs + bi_specs + ws_specs + bs_specs
        + [_resident((2 * nb - 1, 2 * blk, tn), lambda j, i: (0, 0, j)),
           pl.BlockSpec((1, tn), lambda j, i: (0, j)),
           _resident(fwd.shape, lambda j, i: (0, 0)),
           _resident(inv.shape, lambda j, i: (0, 0))],
        out_specs=pl.BlockSpec((1, seq, tn), lambda j, i: (i, 0, j)),
        out_shape=jax.ShapeDtypeStruct((b, seq, d), BF16),
        scratch_shapes=[pltpu.VMEM((seq, tn), F32), pltpu.VMEM((seq, tn), F32),
                        pltpu.VMEM((nb, 2 * blk, tn), F32), pltpu.VMEM((2 * blk, tn), BF16)],
        compiler_params=pltpu.CompilerParams(
            dimension_semantics=("arbitrary", "arbitrary"),
            vmem_limit_bytes=_vmem_limit(V7X_VMEM_BYTES)),
        name="hy_mix",
    )(xn, w_in, w_in, w_in, b_in, b_in, b_in, w_short, w_short, w_short, b_short, b_short, b_short,
      gspec, skip, fwd, inv)


def _filter_kernel(z_ref, w1_ref, b1_ref, f1_ref, w2_ref, b2_ref, f2_ref, w3_ref, b3_ref, f3_ref,
                   w4f_ref, w4b_ref, df_ref, fwd_ref, eneg_ref, g_ref, hf_ref, *, nb, blk):
    seq = z_ref.shape[0]

    @pl.when(pl.program_id(0) == 0)
    def _():
        hf = jnp.sin(f1_ref[...] * (jnp.dot(z_ref[...], w1_ref[...], preferred_element_type=F32) + b1_ref[...]))
        hf = jnp.sin(f2_ref[...] * (jnp.dot(hf, w2_ref[...], preferred_element_type=F32) + b2_ref[...]))
        hf_ref[...] = jnp.sin(f3_ref[...] * (jnp.dot(hf, w3_ref[...], preferred_element_type=F32) + b3_ref[...]))

    hf = hf_ref[...]
    row = lax.broadcasted_iota(jnp.int32, (seq, 1), 0)
    t = row.astype(F32) * (1.0 / (seq - 1))
    decay = jnp.exp(-t * df_ref[...])
    kf = jnp.dot(hf, w4f_ref[...], preferred_element_type=F32) * decay
    kb = jnp.dot(hf, w4b_ref[...], preferred_element_type=F32) * decay
    kb = jnp.where(row == 0, 0.0, kb)

    fwd = fwd_ref[...]
    eneg = eneg_ref[...]

    def spectra(k):
        pos, neg = [], []
        for jb in range(nb):
            kj = k[jb * blk:(jb + 1) * blk, :].astype(BF16)
            pos.append(jnp.dot(fwd, kj, preferred_element_type=F32))
            neg.append(jnp.dot(eneg, kj, preferred_element_type=F32) if jb < nb - 1 else None)
        return pos, neg

    fpos, fneg = spectra(kf)
    bpos, bneg = spectra(kb)

    def store(idx, val, conj_add=None):
        re, im = val[:blk], val[blk:]
        if conj_add is not None:
            re, im = re + conj_add[:blk], im - conj_add[blk:]
        g_ref[idx, 0:blk, :] = re
        g_ref[idx, blk:, :] = im

    store(nb - 1, fpos[0], conj_add=bpos[0])
    for c in range(1, nb):
        store(nb - 1 + c, fneg[c - 1] + fpos[c])
        gb = bneg[c - 1] + bpos[c]
        g_ref[nb - 1 - c, 0:blk, :] = gb[:blk]
        g_ref[nb - 1 - c, blk:, :] = -gb[blk:]


def _filter_spectra(z, w1, b1, f1, w2, b2, f2, w3, b3, f3, w4, deltas, fwd, eneg, *, td, nb, blk):
    seq = z.shape[0]
    d = deltas.shape[1]
    nt = d // td
    order = w2.shape[0]
    full = lambda a: _resident(a.shape, lambda j: (0,) * a.ndim)
    return pl.pallas_call(
        functools.partial(_filter_kernel, nb=nb, blk=blk),
        grid=(nt,),
        in_specs=[full(z), full(w1), full(b1), full(f1), full(w2), full(b2), full(f2), full(w3), full(b3), full(f3),
                  pl.BlockSpec((order, td), lambda j: (0, j)),
                  pl.BlockSpec((order, td), lambda j: (0, nt + j)),
                  pl.BlockSpec((1, td), lambda j: (0, j)),
                  full(fwd), full(eneg)],
        out_specs=pl.BlockSpec((2 * nb - 1, 2 * blk, td), lambda j: (0, 0, j)),
        out_shape=jax.ShapeDtypeStruct((2 * nb - 1, 2 * blk, d), F32),
        scratch_shapes=[pltpu.VMEM((seq, order), F32)],
        compiler_params=pltpu.CompilerParams(
            dimension_semantics=("arbitrary",),
            vmem_limit_bytes=_vmem_limit(2 * (2 * nb - 1) * 2 * blk * td * 4 + (4 * nb) * 2 * blk * td * 4
                                         + 6 * seq * td * 4 + (8 << 20))),
        name="hy_filter",
    )(z, w1, b1, f1, w2, b2, f2, w3, b3, f3, w4, w4, deltas, fwd, eneg)


def _proj_res_kernel(a_ref, x_ref, w_ref, b_ref, o_ref):
    o_ref[...] = x_ref[...] + jnp.dot(a_ref[...], w_ref[...], preferred_element_type=F32) + b_ref[...]


def _proj_res(a, x, w, b, *, tm):
    m, d = x.shape
    return pl.pallas_call(
        _proj_res_kernel,
        grid=(m // tm,),
        in_specs=[pl.BlockSpec((tm, d), lambda i: (i, 0)),
                  pl.BlockSpec((tm, d), lambda i: (i, 0)),
                  _resident((d, d), lambda i: (0, 0)),
                  _resident((1, d), lambda i: (0, 0))],
        out_specs=pl.BlockSpec((tm, d), lambda i: (i, 0)),
        out_shape=jax.ShapeDtypeStruct((m, d), F32),
        compiler_params=pltpu.CompilerParams(
            dimension_semantics=("arbitrary",),
            vmem_limit_bytes=_vmem_limit(d * d * 2 + 7 * tm * d * 4 + (4 << 20))),
        name="hy_out",
    )(a, x, w, b)


def _dft_angles(blk):
    f = (np.arange(blk, dtype=np.float64) + 0.5)[:, None]
    return f * (2.0 * np.pi / (2 * blk))


def _dft_tables(blk):
    w = _dft_angles(blk)
    q = np.arange(blk, dtype=np.float64)[None, :]
    ang = w * q
    fwd = np.concatenate([np.cos(ang), -np.sin(ang)], axis=0)
    ang_n = w * (q - blk)
    eneg = np.concatenate([np.cos(ang_n), -np.sin(ang_n)], axis=0)
    eneg[:, 0] = 0.0
    inv = np.concatenate([np.cos(ang).T, -np.sin(ang).T], axis=1) / blk
    return (jnp.asarray(fwd, dtype=BF16), jnp.asarray(eneg, dtype=BF16), jnp.asarray(inv, dtype=BF16))


def _dw_tables(blk, width, taps_pad):
    w = _dft_angles(blk)
    pad = (width - 1) // 2
    tab = np.zeros((3, 2 * blk, taps_pad), dtype=np.float64)
    for ci, c in enumerate((-1, 0, 1)):
        for j in range(width):
            e = pad - j - blk * c
            if abs(e) <= blk - 1:
                tab[ci, :blk, j] = np.cos(w[:, 0] * e)
                tab[ci, blk:, j] = -np.sin(w[:, 0] * e)
    return jnp.asarray(tab, dtype=F32)


def _hyena_positional_features(seq, pad_to):
    t = jnp.linspace(0.0, 1.0, seq, dtype=F32)[:, None]
    bands = jnp.linspace(1e-4, HYENA_N_BANDS - 1, HYENA_N_BANDS, dtype=F32)[None, :]
    wpos = (2.0 * math.pi) * jnp.arange(seq, dtype=F32)[:, None] / seq
    z = jnp.concatenate([t, jnp.cos(bands * wpos), -jnp.sin(bands * wpos)], axis=-1)
    return jnp.pad(z, ((0, 0), (0, pad_to - z.shape[1])))


def _hyena_decay_rates(d):
    max_decay = math.log(HYENA_DECAY_TARGET) / HYENA_FAST_DECAY_PCT
    min_decay = math.log(HYENA_DECAY_TARGET) / HYENA_SLOW_DECAY_PCT
    return jnp.abs(jnp.linspace(min_decay, max_decay, d, dtype=F32))[None, :]


def _pick(n, prefs):
    for p in prefs:
        if n % p == 0:
            return p
    return n


def kernel(x, norm_mix, norm_ffn, cv_w_pw1, cv_b_pw1, cv_w_dw, cv_b_dw, cv_ln_g, cv_ln_b, cv_w_pw2, cv_b_pw2, hy_w_in, hy_b_in, hy_w_short, hy_b_short, hy_f_w1, hy_f_b1, hy_f_freq1, hy_f_w2, hy_f_b2, hy_f_freq2, hy_f_w3, hy_f_b3, hy_f_freq3, hy_f_w4, hy_skip, hy_w_out, hy_b_out, ffn_w_gate, ffn_w_up, ffn_w_down, norm_final):
    bsz, seq, d = x.shape
    m = bsz * seq
    f = ffn_w_gate.shape[-1]
    depth = norm_mix.shape[0]
    assert depth == 2 and cv_w_pw1.shape[0] == 1 and hy_w_in.shape[0] == 1

    tm = _pick(m, (512, 256, 128))
    tf = _pick(f, (512, 256, 128))
    tn_col = _pick(d, (256, 128))
    td = _pick(d, (512, 256, 128))
    blk = _pick(seq, (CONV_BLOCK, 256, 128))
    nb = seq // blk
    dw_blk = _pick(seq, (DW_BLOCK, 128))
    width = cv_w_dw.shape[1]
    assert (width - 1) // 2 < dw_blk

    row = lambda a: a.reshape(1, -1).astype(F32)
    bf = lambda a: a.astype(BF16)

    h = x.reshape(m, d)

    taps_pad = -(-width // V7X_SUBLANES) * V7X_SUBLANES
    wdw = jnp.pad(cv_w_dw[0], ((0, taps_pad - width), (0, 0)))
    dw_fwd, _, dw_inv = _dft_tables(dw_blk)
    dw_spec = _dw_spectra(_dw_tables(dw_blk, width, taps_pad), wdw, td=td)
    y = _conf_in(x, row(norm_mix[0]), bf(cv_w_pw1[0]), row(cv_b_pw1[0]), dw_spec, row(cv_b_dw[0]),
                 dw_fwd, dw_inv, tn=tn_col, blk=dw_blk)
    h = _conf_out(y.reshape(m, d), h, row(cv_ln_g[0]), row(cv_ln_b[0]), bf(cv_w_pw2[0]), row(cv_b_pw2[0]), tm=tm)
    wg, wu, wd = bf(ffn_w_gate), bf(ffn_w_up), bf(ffn_w_down)
    h, hn = _ffn(h, row(norm_ffn[0]), wg, wu, wd, row(norm_mix[1]), layer=0, tm=tm, tf=tf, final_norm=False)

    fwd, eneg, inv = _dft_tables(blk)
    emb = hy_f_w1.shape[1]
    emb_pad = -(-emb // V7X_LANES) * V7X_LANES
    z = _hyena_positional_features(seq, emb_pad)
    w1 = jnp.pad(hy_f_w1[0], ((0, emb_pad - emb), (0, 0)))
    gspec = _filter_spectra(z, w1, row(hy_f_b1[0]), row(hy_f_freq1[0]), hy_f_w2[0], row(hy_f_b2[0]),
                            row(hy_f_freq2[0]), hy_f_w3[0], row(hy_f_b3[0]), row(hy_f_freq3[0]), hy_f_w4[0],
                            _hyena_decay_rates(d), fwd, eneg, td=td, nb=nb, blk=blk)
    gated = _hy_mix(hn.reshape(bsz, seq, d), bf(hy_w_in[0]), row(hy_b_in[0]), hy_w_short[0], row(hy_b_short[0]),
                    gspec, row(hy_skip[0]), fwd, inv, tn=tn_col, nb=nb, blk=blk)
    h = _proj_res(gated.reshape(m, d), h, bf(hy_w_out[0]), row(hy_b_out[0]), tm=tm)
    h = _ffn(h, row(norm_ffn[1]), wg, wu, wd, row(norm_final), layer=1, tm=tm, tf=tf, final_norm=True)
    return h.reshape(bsz, seq, d)
```

```python
import functools
import math

import jax
import jax.numpy as jnp
import numpy as np
from jax import lax
from jax.experimental import pallas as pl
from jax.experimental.pallas import tpu as pltpu

NORM_EPS = 1e-6
LN_EPS = 1e-5
HYENA_N_BANDS = 16
HYENA_FAST_DECAY_PCT = 0.3
HYENA_SLOW_DECAY_PCT = 1.5
HYENA_DECAY_TARGET = 1e-2

V7X_LANES = 128
V7X_SUBLANES = 8
V7X_BF16_SUBLANES = 16
V7X_VMEM_BYTES = 64 * 1024 * 1024

DW_BLOCK = 256
CONV_BLOCK = 512

F32 = jnp.float32
BF16 = jnp.bfloat16


def _vmem_limit(nbytes):
    return int(min(max(nbytes, 32 * 1024 * 1024), V7X_VMEM_BYTES - 4 * 1024 * 1024))


def _resident(block_shape, index_map):
    return pl.BlockSpec(block_shape, index_map, pipeline_mode=pl.Buffered(1))


def _rmsnorm_bf16(x, g):
    ms = jnp.mean(x * x, axis=-1, keepdims=True)
    return (x * lax.rsqrt(ms + NORM_EPS) * g).astype(BF16)


def _spectral_mix(g_ref, ph_ref, yh_ref, a, *, nb, blk, lag_blocks):
    lanes = yh_ref.shape[-1]
    slab = V7X_BF16_SUBLANES
    lane_w = min(lanes, 2 * V7X_LANES)
    for s in range(blk // slab):
        r_re = slice(s * slab, (s + 1) * slab)
        r_im = slice(blk + s * slab, blk + (s + 1) * slab)
        for lc in range(lanes // lane_w):
            ls = slice(lc * lane_w, (lc + 1) * lane_w)
            acc_re = acc_im = None
            for b in range(max(0, a - lag_blocks), min(nb, a + lag_blocks + 1)):
                gi = a - b + lag_blocks
                gre, gim = g_ref[gi, r_re, ls], g_ref[gi, r_im, ls]
                pre, pim = ph_ref[b, r_re, ls], ph_ref[b, r_im, ls]
                t_re = gre * pre - gim * pim
                t_im = gre * pim + gim * pre
                acc_re = t_re if acc_re is None else acc_re + t_re
                acc_im = t_im if acc_im is None else acc_im + t_im
            yh_ref[r_re, ls] = acc_re.astype(yh_ref.dtype)
            yh_ref[r_im, ls] = acc_im.astype(yh_ref.dtype)


def _dw_spectra_kernel(e_ref, w_ref, o_ref):
    for c in range(e_ref.shape[0]):
        o_ref[c] = jnp.dot(e_ref[c], w_ref[...], preferred_element_type=F32)


def _dw_spectra(etab, wdw, *, td):
    nc, rows, taps = etab.shape
    d = wdw.shape[1]
    return pl.pallas_call(
        _dw_spectra_kernel,
        grid=(d // td,),
        in_specs=[_resident(etab.shape, lambda j: (0, 0, 0)), pl.BlockSpec((taps, td), lambda j: (0, j))],
        out_specs=pl.BlockSpec((nc, rows, td), lambda j: (0, 0, j)),
        out_shape=jax.ShapeDtypeStruct((nc, rows, d), F32),
        compiler_params=pltpu.CompilerParams(dimension_semantics=("arbitrary",)),
        name="dw_spectra",
    )(etab, wdw)


def _conf_in_kernel(x_ref, g_ref, wa_ref, wb_ref, ba_ref, bb_ref, gs_ref, bdw_ref, fwd_ref, inv_ref,
                    y_ref, xn_ref, ph_ref, yh_ref, *, nb, blk):
    @pl.when(pl.program_id(1) == 0)
    def _():
        xn_ref[...] = _rmsnorm_bf16(x_ref[0], g_ref[...])

    xn = xn_ref[...]
    a1 = jnp.dot(xn, wa_ref[...], preferred_element_type=F32) + ba_ref[...]
    a2 = jnp.dot(xn, wb_ref[...], preferred_element_type=F32) + bb_ref[...]
    u = (a1 * jax.nn.sigmoid(a2)).astype(BF16)
    for b in range(nb):
        ph_ref[b] = jnp.dot(fwd_ref[...], u[b * blk:(b + 1) * blk, :], preferred_element_type=F32)
    for a in range(nb):
        _spectral_mix(gs_ref, ph_ref, yh_ref, a, nb=nb, blk=blk, lag_blocks=1)
        y = jnp.dot(inv_ref[...], yh_ref[...], preferred_element_type=F32)
        y_ref[0, a * blk:(a + 1) * blk, :] = y + bdw_ref[...]


def _conf_in(x, g, w, b, gs, bdw, fwd, inv, *, tn, blk):
    bsz, seq, d = x.shape
    nt = d // tn
    nb = seq // blk
    return pl.pallas_call(
        functools.partial(_conf_in_kernel, nb=nb, blk=blk),
        grid=(bsz, nt),
        in_specs=[
            _resident((1, seq, d), lambda i, j: (i, 0, 0)),
            _resident((1, d), lambda i, j: (0, 0)),
            pl.BlockSpec((d, tn), lambda i, j: (0, j)),
            pl.BlockSpec((d, tn), lambda i, j: (0, nt + j)),
            pl.BlockSpec((1, tn), lambda i, j: (0, j)),
            pl.BlockSpec((1, tn), lambda i, j: (0, nt + j)),
            pl.BlockSpec((gs.shape[0], 2 * blk, tn), lambda i, j: (0, 0, j)),
            pl.BlockSpec((1, tn), lambda i, j: (0, j)),
            _resident(fwd.shape, lambda i, j: (0, 0)),
            _resident(inv.shape, lambda i, j: (0, 0)),
        ],
        out_specs=pl.BlockSpec((1, seq, tn), lambda i, j: (i, 0, j)),
        out_shape=jax.ShapeDtypeStruct((bsz, seq, d), F32),
        scratch_shapes=[pltpu.VMEM((seq, d), BF16), pltpu.VMEM((nb, 2 * blk, tn), F32),
                        pltpu.VMEM((2 * blk, tn), BF16)],
        compiler_params=pltpu.CompilerParams(
            dimension_semantics=("arbitrary", "arbitrary"),
            vmem_limit_bytes=_vmem_limit(seq * d * 6 + 8 * d * tn + 14 * seq * tn * 4 + (6 << 20))),
        name="conf_in",
    )(x, g, w, w, b, b, gs, bdw, fwd, inv)


def _conf_out_kernel(y_ref, x_ref, lg_ref, lb_ref, w2_ref, b2_ref, o_ref):
    y = y_ref[...]
    mu = jnp.mean(y, axis=-1, keepdims=True)
    yc = y - mu
    var = jnp.mean(yc * yc, axis=-1, keepdims=True)
    z = yc * lax.rsqrt(var + LN_EPS) * lg_ref[...] + lb_ref[...]
    act = (z * jax.nn.sigmoid(z)).astype(BF16)
    o_ref[...] = x_ref[...] + jnp.dot(act, w2_ref[...], preferred_element_type=F32) + b2_ref[...]


def _conf_out(y, x, lg, lb, w2, b2, *, tm):
    m, d = y.shape
    return pl.pallas_call(
        _conf_out_kernel,
        grid=(m // tm,),
        in_specs=[
            pl.BlockSpec((tm, d), lambda i: (i, 0)),
            pl.BlockSpec((tm, d), lambda i: (i, 0)),
            _resident((1, d), lambda i: (0, 0)),
            _resident((1, d), lambda i: (0, 0)),
            _resident((d, d), lambda i: (0, 0)),
            _resident((1, d), lambda i: (0, 0)),
        ],
        out_specs=pl.BlockSpec((tm, d), lambda i: (i, 0)),
        out_shape=jax.ShapeDtypeStruct((m, d), F32),
        compiler_params=pltpu.CompilerParams(
            dimension_semantics=("arbitrary",),
            vmem_limit_bytes=_vmem_limit(d * d * 2 + 12 * tm * d * 4 + (4 << 20))),
        name="conf_out",
    )(y, x, lg, lb, w2, b2)


def _ffn_kernel(x_ref, g_ref, wg_ref, wu_ref, wd_ref, gn_ref, o_ref, *rest, final_norm):
    xn_ref = rest[-1]
    j = pl.program_id(1)

    @pl.when(j == 0)
    def _():
        x = x_ref[...]
        xn_ref[...] = _rmsnorm_bf16(x, g_ref[...])
        o_ref[...] = x

    xn = xn_ref[...]
    gate = jnp.dot(xn, wg_ref[...], preferred_element_type=F32)
    up = jnp.dot(xn, wu_ref[...], preferred_element_type=F32)
    act = (gate * jax.nn.sigmoid(gate) * up).astype(BF16)
    o_ref[...] += jnp.dot(act, wd_ref[...], preferred_element_type=F32)

    @pl.when(j == pl.num_programs(1) - 1)
    def _():
        h = o_ref[...]
        ms = jnp.mean(h * h, axis=-1, keepdims=True)
        hn = h * lax.rsqrt(ms + NORM_EPS) * gn_ref[...]
        if final_norm:
            o_ref[...] = hn
        else:
            rest[0][...] = hn.astype(BF16)


def _ffn(x, g, wg, wu, wd, gn, *, layer, tm, tf, final_norm):
    m, d = x.shape
    f = wg.shape[-1]
    row_spec = pl.BlockSpec((tm, d), lambda i, j: (i, 0))
    out_specs, out_shape = row_spec, jax.ShapeDtypeStruct((m, d), F32)
    if not final_norm:
        out_specs = [row_spec, row_spec]
        out_shape = [out_shape, jax.ShapeDtypeStruct((m, d), BF16)]
    return pl.pallas_call(
        functools.partial(_ffn_kernel, final_norm=final_norm),
        grid=(m // tm, f // tf),
        in_specs=[
            row_spec,
            _resident((1, d), lambda i, j: (0, 0)),
            pl.BlockSpec((None, d, tf), lambda i, j: (layer, 0, j)),
            pl.BlockSpec((None, d, tf), lambda i, j: (layer, 0, j)),
            pl.BlockSpec((None, tf, d), lambda i, j: (layer, j, 0)),
            _resident((1, d), lambda i, j: (0, 0)),
        ],
        out_specs=out_specs,
        out_shape=out_shape,
        scratch_shapes=[pltpu.VMEM((tm, d), BF16)],
        compiler_params=pltpu.CompilerParams(
            dimension_semantics=("arbitrary", "arbitrary"),
            vmem_limit_bytes=_vmem_limit(5 * tm * d * 4 + 3 * tm * d * 2 + 12 * d * tf + 5 * tm * tf * 4 + (4 << 20))),
        name="ffn_final" if final_norm else "ffn",
    )(x, g, wg, wu, wd, gn)


def _hy_mix_kernel(xn_ref, w0_ref, w1_ref, w2_ref, bi0_ref, bi1_ref, bi2_ref,
                   ws0_ref, ws1_ref, ws2_ref, bs0_ref, bs1_ref, bs2_ref,
                   g_ref, skip_ref, fwd_ref, inv_ref, o_ref, z_ref, p_ref, x0_ref, ph_ref, yh_ref, *, nb, blk):
    seq = xn_ref.shape[1]
    halo = V7X_SUBLANES
    branches = ((w0_ref, bi0_ref, ws0_ref, bs0_ref), (w1_ref, bi1_ref, ws1_ref, bs1_ref),
                (w2_ref, bi2_ref, ws2_ref, bs2_ref))

    def short_conv(k):
        lo, hi = max(k * blk - halo, 0), min((k + 1) * blk + halo, seq)
        n = hi - lo
        off = k * blk - lo
        row = lax.broadcasted_iota(jnp.int32, (n, 1), 0)
        outs = []
        for br, (_, _, ws_ref, bs_ref) in enumerate(branches):
            z = z_ref[br, lo:hi, :]
            zp = pltpu.roll(z, 1, axis=0)
            zn = pltpu.roll(z, n - 1, axis=0)
            if lo == 0:
                zp = jnp.where(row == 0, 0.0, zp)
            if hi == seq:
                zn = jnp.where(row == n - 1, 0.0, zn)
            y = ws_ref[0:1, :] * zp + ws_ref[1:2, :] * z + ws_ref[2:3, :] * zn + bs_ref[...]
            outs.append(y[off:off + blk, :])
        x0_ref[k * blk:(k + 1) * blk, :] = outs[0]
        p_ref[k * blk:(k + 1) * blk, :] = outs[2] * outs[1]

    for c in range(nb):
        rows = slice(c * blk, (c + 1) * blk)
        for br, (w_ref, bi_ref, _, _) in enumerate(branches):
            z_ref[br, rows, :] = jnp.dot(xn_ref[0, rows, :], w_ref[...], preferred_element_type=F32) + bi_ref[...]
        if c >= 1:
            short_conv(c - 1)
    short_conv(nb - 1)

    for b in range(nb):
        pb = p_ref[b * blk:(b + 1) * blk, :].astype(BF16)
        ph_ref[b] = jnp.dot(fwd_ref[...], pb, preferred_element_type=F32)
    for a in range(nb):
        _spectral_mix(g_ref, ph_ref, yh_ref, a, nb=nb, blk=blk, lag_blocks=nb - 1)
        y = jnp.dot(inv_ref[...], yh_ref[...], preferred_element_type=F32)
        rows = slice(a * blk, (a + 1) * blk)
        y = y + p_ref[rows, :] * skip_ref[...]
        o_ref[0, rows, :] = (y * x0_ref[rows, :]).astype(o_ref.dtype)


def _hy_mix(xn, w_in, b_in, w_short, b_short, gspec, skip, fwd, inv, *, tn, nb, blk):
    b, seq, d = xn.shape
    nt = d // tn

    def col(k):
        return lambda j, i: (0, k * nt + j)

    w_specs = [_resident((d, tn), col(k)) for k in range(3)]
    bi_specs = [pl.BlockSpec((1, tn), col(k)) for k in range(3)]
    ws_specs = [pl.BlockSpec((w_short.shape[0], tn), col(k)) for k in range(3)]
    bs_specs = [pl.BlockSpec((1, tn), col(k)) for k in range(3)]
    return pl.pallas_call(
        functools.partial(_hy_mix_kernel, nb=nb, blk=blk),
        grid=(nt, b),
        in_specs=[pl.BlockSpec((1, seq, d), lambda j, i: (i, 0, 0))]
        + w_specs + bi_specs + ws_specs + bs_specs
        + [_resident((2 * nb - 1, 2 * blk, tn), lambda j, i: (0, 0, j)),
           pl.BlockSpec((1, tn), lambda j, i: (0, j)),
           _resident(fwd.shape, lambda j, i: (0, 0)),
           _resident(inv.shape, lambda j, i: (0, 0))],
        out_specs=pl.BlockSpec((1, seq, tn), lambda j, i: (i, 0, j)),
        out_shape=jax.ShapeDtypeStruct((b, seq, d), BF16),
        scratch_shapes=[pltpu.VMEM((3, seq, tn), F32), pltpu.VMEM((seq, tn), F32), pltpu.VMEM((seq, tn), F32),
                        pltpu.VMEM((nb, 2 * blk, tn), F32), pltpu.VMEM((2 * blk, tn), BF16)],
        compiler_params=pltpu.CompilerParams(
            dimension_semantics=("arbitrary", "arbitrary"),
            vmem_limit_bytes=_vmem_limit(V7X_VMEM_BYTES)),
        name="hy_mix",
    )(xn, w_in, w_in, w_in, b_in, b_in, b_in, w_short, w_short, w_short, b_short, b_short, b_short,
      gspec, skip, fwd, inv)


def _filter_kernel(z_ref, w1_ref, b1_ref, f1_ref, w2_ref, b2_ref, f2_ref, w3_ref, b3_ref, f3_ref,
                   w4f_ref, w4b_ref, df_ref, fwd_ref, eneg_ref, g_ref, hf_ref, *, nb, blk):
    seq = z_ref.shape[0]

    @pl.when(pl.program_id(0) == 0)
    def _():
        hf = jnp.sin(f1_ref[...] * (jnp.dot(z_ref[...], w1_ref[...], preferred_element_type=F32) + b1_ref[...]))
        hf = jnp.sin(f2_ref[...] * (jnp.dot(hf, w2_ref[...], preferred_element_type=F32) + b2_ref[...]))
        hf_ref[...] = jnp.sin(f3_ref[...] * (jnp.dot(hf, w3_ref[...], preferred_element_type=F32) + b3_ref[...]))

    hf = hf_ref[...]
    row = lax.broadcasted_iota(jnp.int32, (seq, 1), 0)
    t = row.astype(F32) * (1.0 / (seq - 1))
    decay = jnp.exp(-t * df_ref[...])
    kf = jnp.dot(hf, w4f_ref[...], preferred_element_type=F32) * decay
    kb = jnp.dot(hf, w4b_ref[...], preferred_element_type=F32) * decay
    kb = jnp.where(row == 0, 0.0, kb)

    fwd = fwd_ref[...]
    eneg = eneg_ref[...]

    def spectra(k):
        pos, neg = [], []
        for jb in range(nb):
            kj = k[jb * blk:(jb + 1) * blk, :].astype(BF16)
            pos.append(jnp.dot(fwd, kj, preferred_element_type=F32))
            neg.append(jnp.dot(eneg, kj, preferred_element_type=F32) if jb < nb - 1 else None)
        return pos, neg

    fpos, fneg = spectra(kf)
    bpos, bneg = spectra(kb)

    def store(idx, val, conj_add=None):
        re, im = val[:blk], val[blk:]
        if conj_add is not None:
            re, im = re + conj_add[:blk], im - conj_add[blk:]
        g_ref[idx, 0:blk, :] = re
        g_ref[idx, blk:, :] = im

    store(nb - 1, fpos[0], conj_add=bpos[0])
    for c in range(1, nb):
        store(nb - 1 + c, fneg[c - 1] + fpos[c])
        gb = bneg[c - 1] + bpos[c]
        g_ref[nb - 1 - c, 0:blk, :] = gb[:blk]
        g_ref[nb - 1 - c, blk:, :] = -gb[blk:]


def _filter_spectra(z, w1, b1, f1, w2, b2, f2, w3, b3, f3, w4, deltas, fwd, eneg, *, td, nb, blk):
    seq = z.shape[0]
    d = deltas.shape[1]
    nt = d // td
    order = w2.shape[0]
    full = lambda a: _resident(a.shape, lambda j: (0,) * a.ndim)
    return pl.pallas_call(
        functools.partial(_filter_kernel, nb=nb, blk=blk),
        grid=(nt,),
        in_specs=[full(z), full(w1), full(b1), full(f1), full(w2), full(b2), full(f2), full(w3), full(b3), full(f3),
                  pl.BlockSpec((order, td), lambda j: (0, j)),
                  pl.BlockSpec((order, td), lambda j: (0, nt + j)),
                  pl.BlockSpec((1, td), lambda j: (0, j)),
                  full(fwd), full(eneg)],
        out_specs=pl.BlockSpec((2 * nb - 1, 2 * blk, td), lambda j: (0, 0, j)),
        out_shape=jax.ShapeDtypeStruct((2 * nb - 1, 2 * blk, d), F32),
        scratch_shapes=[pltpu.VMEM((seq, order), F32)],
        compiler_params=pltpu.CompilerParams(
            dimension_semantics=("arbitrary",),
            vmem_limit_bytes=_vmem_limit(2 * (2 * nb - 1) * 2 * blk * td * 4 + (4 * nb) * 2 * blk * td * 4
                                         + 6 * seq * td * 4 + (8 << 20))),
        name="hy_filter",
    )(z, w1, b1, f1, w2, b2, f2, w3, b3, f3, w4, w4, deltas, fwd, eneg)


def _proj_res_kernel(a_ref, x_ref, w_ref, b_ref, o_ref):
    o_ref[...] = x_ref[...] + jnp.dot(a_ref[...], w_ref[...], preferred_element_type=F32) + b_ref[...]


def _proj_res(a, x, w, b, *, tm):
    m, d = x.shape
    return pl.pallas_call(
        _proj_res_kernel,
        grid=(m // tm,),
        in_specs=[pl.BlockSpec((tm, d), lambda i: (i, 0)),
                  pl.BlockSpec((tm, d), lambda i: (i, 0)),
                  _resident((d, d), lambda i: (0, 0)),
                  _resident((1, d), lambda i: (0, 0))],
        out_specs=pl.BlockSpec((tm, d), lambda i: (i, 0)),
        out_shape=jax.ShapeDtypeStruct((m, d), F32),
        compiler_params=pltpu.CompilerParams(
            dimension_semantics=("arbitrary",),
            vmem_limit_bytes=_vmem_limit(d * d * 2 + 7 * tm * d * 4 + (4 << 20))),
        name="hy_out",
    )(a, x, w, b)


def _dft_angles(blk):
    f = (np.arange(blk, dtype=np.float64) + 0.5)[:, None]
    return f * (2.0 * np.pi / (2 * blk))


def _dft_tables(blk):
    w = _dft_angles(blk)
    q = np.arange(blk, dtype=np.float64)[None, :]
    ang = w * q
    fwd = np.concatenate([np.cos(ang), -np.sin(ang)], axis=0)
    ang_n = w * (q - blk)
    eneg = np.concatenate([np.cos(ang_n), -np.sin(ang_n)], axis=0)
    eneg[:, 0] = 0.0
    inv = np.concatenate([np.cos(ang).T, -np.sin(ang).T], axis=1) / blk
    return (jnp.asarray(fwd, dtype=BF16), jnp.asarray(eneg, dtype=BF16), jnp.asarray(inv, dtype=BF16))


def _dw_tables(blk, width, taps_pad):
    w = _dft_angles(blk)
    pad = (width - 1) // 2
    tab = np.zeros((3, 2 * blk, taps_pad), dtype=np.float64)
    for ci, c in enumerate((-1, 0, 1)):
        for j in range(width):
            e = pad - j - blk * c
            if abs(e) <= blk - 1:
                tab[ci, :blk, j] = np.cos(w[:, 0] * e)
                tab[ci, blk:, j] = -np.sin(w[:, 0] * e)
    return jnp.asarray(tab, dtype=F32)


def _hyena_positional_features(seq, pad_to):
    t = jnp.linspace(0.0, 1.0, seq, dtype=F32)[:, None]
    bands = jnp.linspace(1e-4, HYENA_N_BANDS - 1, HYENA_N_BANDS, dtype=F32)[None, :]
    wpos = (2.0 * math.pi) * jnp.arange(seq, dtype=F32)[:, None] / seq
    z = jnp.concatenate([t, jnp.cos(bands * wpos), -jnp.sin(bands * wpos)], axis=-1)
    return jnp.pad(z, ((0, 0), (0, pad_to - z.shape[1])))


def _hyena_decay_rates(d):
    max_decay = math.log(HYENA_DECAY_TARGET) / HYENA_FAST_DECAY_PCT
    min_decay = math.log(HYENA_DECAY_TARGET) / HYENA_SLOW_DECAY_PCT
    return jnp.abs(jnp.linspace(min_decay, max_decay, d, dtype=F32))[None, :]


def _pick(n, prefs):
    for p in prefs:
        if n % p == 0:
            return p
    return n


def kernel(x, norm_mix, norm_ffn, cv_w_pw1, cv_b_pw1, cv_w_dw, cv_b_dw, cv_ln_g, cv_ln_b, cv_w_pw2, cv_b_pw2, hy_w_in, hy_b_in, hy_w_short, hy_b_short, hy_f_w1, hy_f_b1, hy_f_freq1, hy_f_w2, hy_f_b2, hy_f_freq2, hy_f_w3, hy_f_b3, hy_f_freq3, hy_f_w4, hy_skip, hy_w_out, hy_b_out, ffn_w_gate, ffn_w_up, ffn_w_down, norm_final):
    bsz, seq, d = x.shape
    m = bsz * seq
    f = ffn_w_gate.shape[-1]
    depth = norm_mix.shape[0]
    assert depth == 2 and cv_w_pw1.shape[0] == 1 and hy_w_in.shape[0] == 1

    tm = _pick(m, (512, 256, 128))
    tf = _pick(f, (512, 256, 128))
    tn_col = _pick(d, (256, 128))
    td = _pick(d, (512, 256, 128))
    blk = _pick(seq, (CONV_BLOCK, 256, 128))
    nb = seq // blk
    dw_blk = _pick(seq, (DW_BLOCK, 128))
    width = cv_w_dw.shape[1]
    assert (width - 1) // 2 < dw_blk

    row = lambda a: a.reshape(1, -1).astype(F32)
    bf = lambda a: a.astype(BF16)

    h = x.reshape(m, d)

    taps_pad = -(-width // V7X_SUBLANES) * V7X_SUBLANES
    wdw = jnp.pad(cv_w_dw[0], ((0, taps_pad - width), (0, 0)))
    dw_fwd, _, dw_inv = _dft_tables(dw_blk)
    dw_spec = _dw_spectra(_dw_tables(dw_blk, width, taps_pad), wdw, td=td)
    y = _conf_in(x, row(norm_mix[0]), bf(cv_w_pw1[0]), row(cv_b_pw1[0]), dw_spec, row(cv_b_dw[0]),
                 dw_fwd, dw_inv, tn=tn_col, blk=dw_blk)
    h = _conf_out(y.reshape(m, d), h, row(cv_ln_g[0]), row(cv_ln_b[0]), bf(cv_w_pw2[0]), row(cv_b_pw2[0]), tm=tm)
    wg, wu, wd = bf(ffn_w_gate), bf(ffn_w_up), bf(ffn_w_down)
    h, hn = _ffn(h, row(norm_ffn[0]), wg, wu, wd, row(norm_mix[1]), layer=0, tm=tm, tf=tf, final_norm=False)

    fwd, eneg, inv = _dft_tables(blk)
    emb = hy_f_w1.shape[1]
    emb_pad = -(-emb // V7X_LANES) * V7X_LANES
    z = _hyena_positional_features(seq, emb_pad)
    w1 = jnp.pad(hy_f_w1[0], ((0, emb_pad - emb), (0, 0)))
    gspec = _filter_spectra(z, w1, row(hy_f_b1[0]), row(hy_f_freq1[0]), hy_f_w2[0], row(hy_f_b2[0]),
                            row(hy_f_freq2[0]), hy_f_w3[0], row(hy_f_b3[0]), row(hy_f_freq3[0]), hy_f_w4[0],
                            _hyena_decay_rates(d), fwd, eneg, td=td, nb=nb, blk=blk)
    gated = _hy_mix(hn.reshape(bsz, seq, d), bf(hy_w_in[0]), row(hy_b_in[0]), hy_w_short[0], row(hy_b_short[0]),
                    gspec, row(hy_skip[0]), fwd, inv, tn=tn_col, nb=nb, blk=blk)
    h = _proj_res(gated.reshape(m, d), h, bf(hy_w_out[0]), row(hy_b_out[0]), tm=tm)
    h = _ffn(h, row(norm_ffn[1]), wg, wu, wd, row(norm_final), layer=1, tm=tm, tf=tf, final_norm=True)
    return h.reshape(bsz, seq, d)
```

```python
import functools
import math

import jax
import jax.numpy as jnp
import numpy as np
from jax import lax
from jax.experimental import pallas as pl
from jax.experimental.pallas import tpu as pltpu

NORM_EPS = 1e-6
LN_EPS = 1e-5
HYENA_N_BANDS = 16
HYENA_FAST_DECAY_PCT = 0.3
HYENA_SLOW_DECAY_PCT = 1.5
HYENA_DECAY_TARGET = 1e-2

V7X_LANES = 128
V7X_SUBLANES = 8
V7X_BF16_SUBLANES = 16
V7X_VMEM_BYTES = 64 * 1024 * 1024

DW_BLOCK = 256
CONV_BLOCK = 512

F32 = jnp.float32
BF16 = jnp.bfloat16


def _vmem_limit(nbytes):
    return int(min(max(nbytes, 32 * 1024 * 1024), V7X_VMEM_BYTES - 4 * 1024 * 1024))


def _resident(block_shape, index_map):
    return pl.BlockSpec(block_shape, index_map, pipeline_mode=pl.Buffered(1))


def _rmsnorm_bf16(x, g):
    ms = jnp.mean(x * x, axis=-1, keepdims=True)
    return (x * lax.rsqrt(ms + NORM_EPS) * g).astype(BF16)


def _spectral_mix(g_ref, ph_ref, yh_ref, a, *, nb, blk, lag_blocks):
    lanes = yh_ref.shape[-1]
    slab = V7X_BF16_SUBLANES
    lane_w = min(lanes, 2 * V7X_LANES)
    for s in range(blk // slab):
        r_re = slice(s * slab, (s + 1) * slab)
        r_im = slice(blk + s * slab, blk + (s + 1) * slab)
        for lc in range(lanes // lane_w):
            ls = slice(lc * lane_w, (lc + 1) * lane_w)
            acc_re = acc_im = None
            for b in range(max(0, a - lag_blocks), min(nb, a + lag_blocks + 1)):
                gi = a - b + lag_blocks
                gre, gim = g_ref[gi, r_re, ls], g_ref[gi, r_im, ls]
                pre, pim = ph_ref[b, r_re, ls], ph_ref[b, r_im, ls]
                t_re = gre * pre - gim * pim
                t_im = gre * pim + gim * pre
                acc_re = t_re if acc_re is None else acc_re + t_re
                acc_im = t_im if acc_im is None else acc_im + t_im
            yh_ref[r_re, ls] = acc_re.astype(yh_ref.dtype)
            yh_ref[r_im, ls] = acc_im.astype(yh_ref.dtype)


def _dw_spectra_kernel(e_ref, w_ref, o_ref):
    for c in range(e_ref.shape[0]):
        o_ref[c] = jnp.dot(e_ref[c], w_ref[...], preferred_element_type=F32)


def _dw_spectra(etab, wdw, *, td):
    nc, rows, taps = etab.shape
    d = wdw.shape[1]
    return pl.pallas_call(
        _dw_spectra_kernel,
        grid=(d // td,),
        in_specs=[_resident(etab.shape, lambda j: (0, 0, 0)), pl.BlockSpec((taps, td), lambda j: (0, j))],
        out_specs=pl.BlockSpec((nc, rows, td), lambda j: (0, 0, j)),
        out_shape=jax.ShapeDtypeStruct((nc, rows, d), F32),
        compiler_params=pltpu.CompilerParams(dimension_semantics=("arbitrary",)),
        name="dw_spectra",
    )(etab, wdw)


def _conf_in_kernel(x_ref, g_ref, wa_ref, wb_ref, ba_ref, bb_ref, gs_ref, bdw_ref, fwd_ref, inv_ref,
                    y_ref, xn_ref, ph_ref, yh_ref, *, nb, blk):
    @pl.when(pl.program_id(1) == 0)
    def _():
        xn_ref[...] = _rmsnorm_bf16(x_ref[0], g_ref[...])

    per_chunk = max(1, min(nb, CONV_BLOCK // blk))
    assert nb % per_chunk == 0
    for c in range(nb // per_chunk):
        rows = slice(c * per_chunk * blk, (c + 1) * per_chunk * blk)
        a1 = jnp.dot(xn_ref[rows, :], wa_ref[...], preferred_element_type=F32) + ba_ref[...]
        a2 = jnp.dot(xn_ref[rows, :], wb_ref[...], preferred_element_type=F32) + bb_ref[...]
        u = (a1 * jax.nn.sigmoid(a2)).astype(BF16)
        for k in range(per_chunk):
            ph_ref[c * per_chunk + k] = jnp.dot(fwd_ref[...], u[k * blk:(k + 1) * blk, :],
                                                preferred_element_type=F32)
    for a in range(nb):
        _spectral_mix(gs_ref, ph_ref, yh_ref, a, nb=nb, blk=blk, lag_blocks=1)
        y = jnp.dot(inv_ref[...], yh_ref[...], preferred_element_type=F32)
        y_ref[0, a * blk:(a + 1) * blk, :] = y + bdw_ref[...]


def _conf_in(x, g, w, b, gs, bdw, fwd, inv, *, tn, blk):
    bsz, seq, d = x.shape
    nt = d // tn
    nb = seq // blk
    return pl.pallas_call(
        functools.partial(_conf_in_kernel, nb=nb, blk=blk),
        grid=(bsz, nt),
        in_specs=[
            _resident((1, seq, d), lambda i, j: (i, 0, 0)),
            _resident((1, d), lambda i, j: (0, 0)),
            pl.BlockSpec((d, tn), lambda i, j: (0, j)),
            pl.BlockSpec((d, tn), lambda i, j: (0, nt + j)),
            pl.BlockSpec((1, tn), lambda i, j: (0, j)),
            pl.BlockSpec((1, tn), lambda i, j: (0, nt + j)),
            pl.BlockSpec((gs.shape[0], 2 * blk, tn), lambda i, j: (0, 0, j)),
            pl.BlockSpec((1, tn), lambda i, j: (0, j)),
            _resident(fwd.shape, lambda i, j: (0, 0)),
            _resident(inv.shape, lambda i, j: (0, 0)),
        ],
        out_specs=pl.BlockSpec((1, seq, tn), lambda i, j: (i, 0, j)),
        out_shape=jax.ShapeDtypeStruct((bsz, seq, d), F32),
        scratch_shapes=[pltpu.VMEM((seq, d), BF16), pltpu.VMEM((nb, 2 * blk, tn), F32),
                        pltpu.VMEM((2 * blk, tn), BF16)],
        compiler_params=pltpu.CompilerParams(
            dimension_semantics=("arbitrary", "arbitrary"),
            vmem_limit_bytes=_vmem_limit(seq * d * 6 + 8 * d * tn + 14 * seq * tn * 4 + (6 << 20))),
        name="conf_in",
    )(x, g, w, w, b, b, gs, bdw, fwd, inv)


def _conf_out_kernel(y_ref, x_ref, lg_ref, lb_ref, w2_ref, b2_ref, o_ref):
    y = y_ref[...]
    mu = jnp.mean(y, axis=-1, keepdims=True)
    yc = y - mu
    var = jnp.mean(yc * yc, axis=-1, keepdims=True)
    z = yc * lax.rsqrt(var + LN_EPS) * lg_ref[...] + lb_ref[...]
    act = (z * jax.nn.sigmoid(z)).astype(BF16)
    o_ref[...] = x_ref[...] + jnp.dot(act, w2_ref[...], preferred_element_type=F32) + b2_ref[...]


def _conf_out(y, x, lg, lb, w2, b2, *, tm):
    m, d = y.shape
    return pl.pallas_call(
        _conf_out_kernel,
        grid=(m // tm,),
        in_specs=[
            pl.BlockSpec((tm, d), lambda i: (i, 0)),
            pl.BlockSpec((tm, d), lambda i: (i, 0)),
            _resident((1, d), lambda i: (0, 0)),
            _resident((1, d), lambda i: (0, 0)),
            _resident((d, d), lambda i: (0, 0)),
            _resident((1, d), lambda i: (0, 0)),
        ],
        out_specs=pl.BlockSpec((tm, d), lambda i: (i, 0)),
        out_shape=jax.ShapeDtypeStruct((m, d), F32),
        compiler_params=pltpu.CompilerParams(
            dimension_semantics=("arbitrary",),
            vmem_limit_bytes=_vmem_limit(d * d * 2 + 12 * tm * d * 4 + (4 << 20))),
        name="conf_out",
    )(y, x, lg, lb, w2, b2)


def _ffn_kernel(x_ref, g_ref, wg_ref, wu_ref, wd_ref, gn_ref, o_ref, *rest, final_norm):
    xn_ref = rest[-1]
    j = pl.program_id(1)

    @pl.when(j == 0)
    def _():
        x = x_ref[...]
        xn_ref[...] = _rmsnorm_bf16(x, g_ref[...])
        o_ref[...] = x

    xn = xn_ref[...]
    gate = jnp.dot(xn, wg_ref[...], preferred_element_type=F32)
    up = jnp.dot(xn, wu_ref[...], preferred_element_type=F32)
    act = (gate * jax.nn.sigmoid(gate) * up).astype(BF16)
    o_ref[...] += jnp.dot(act, wd_ref[...], preferred_element_type=F32)

    @pl.when(j == pl.num_programs(1) - 1)
    def _():
        h = o_ref[...]
        ms = jnp.mean(h * h, axis=-1, keepdims=True)
        hn = h * lax.rsqrt(ms + NORM_EPS) * gn_ref[...]
        if final_norm:
            o_ref[...] = hn
        else:
            rest[0][...] = hn.astype(BF16)


def _ffn(x, g, wg, wu, wd, gn, *, layer, tm, tf, final_norm):
    m, d = x.shape
    f = wg.shape[-1]
    row_spec = pl.BlockSpec((tm, d), lambda i, j: (i, 0))
    out_specs, out_shape = row_spec, jax.ShapeDtypeStruct((m, d), F32)
    if not final_norm:
        out_specs = [row_spec, row_spec]
        out_shape = [out_shape, jax.ShapeDtypeStruct((m, d), BF16)]
    return pl.pallas_call(
        functools.partial(_ffn_kernel, final_norm=final_norm),
        grid=(m // tm, f // tf),
        in_specs=[
            row_spec,
            _resident((1, d), lambda i, j: (0, 0)),
            pl.BlockSpec((None, d, tf), lambda i, j: (layer, 0, j)),
            pl.BlockSpec((None, d, tf), lambda i, j: (layer, 0, j)),
            pl.BlockSpec((None, tf, d), lambda i, j: (layer, j, 0)),
            _resident((1, d), lambda i, j: (0, 0)),
        ],
        out_specs=out_specs,
        out_shape=out_shape,
        scratch_shapes=[pltpu.VMEM((tm, d), BF16)],
        compiler_params=pltpu.CompilerParams(
            dimension_semantics=("arbitrary", "arbitrary"),
            vmem_limit_bytes=_vmem_limit(5 * tm * d * 4 + 3 * tm * d * 2 + 12 * d * tf + 5 * tm * tf * 4 + (4 << 20))),
        name="ffn_final" if final_norm else "ffn",
    )(x, g, wg, wu, wd, gn)


def _hy_mix_kernel(xn_ref, w0_ref, w1_ref, w2_ref, bi0_ref, bi1_ref, bi2_ref,
                   ws0_ref, ws1_ref, ws2_ref, bs0_ref, bs1_ref, bs2_ref,
                   g_ref, skip_ref, fwd_ref, inv_ref, o_ref, z_ref, p_ref, x0_ref, ph_ref, yh_ref, *, nb, blk):
    seq = xn_ref.shape[1]
    halo = V7X_SUBLANES
    branches = ((w0_ref, bi0_ref, ws0_ref, bs0_ref), (w1_ref, bi1_ref, ws1_ref, bs1_ref),
                (w2_ref, bi2_ref, ws2_ref, bs2_ref))

    def short_conv(k):
        lo, hi = max(k * blk - halo, 0), min((k + 1) * blk + halo, seq)
        n = hi - lo
        off = k * blk - lo
        row = lax.broadcasted_iota(jnp.int32, (n, 1), 0)
        outs = []
        for br, (_, _, ws_ref, bs_ref) in enumerate(branches):
            z = z_ref[br, lo:hi, :]
            zp = pltpu.roll(z, 1, axis=0)
            zn = pltpu.roll(z, n - 1, axis=0)
            if lo == 0:
                zp = jnp.where(row == 0, 0.0, zp)
            if hi == seq:
                zn = jnp.where(row == n - 1, 0.0, zn)
            y = ws_ref[0:1, :] * zp + ws_ref[1:2, :] * z + ws_ref[2:3, :] * zn + bs_ref[...]
            outs.append(y[off:off + blk, :])
        x0_ref[k * blk:(k + 1) * blk, :] = outs[0]
        p_ref[k * blk:(k + 1) * blk, :] = outs[2] * outs[1]

    for c in range(nb):
        rows = slice(c * blk, (c + 1) * blk)
        for br, (w_ref, bi_ref, _, _) in enumerate(branches):
            z_ref[br, rows, :] = jnp.dot(xn_ref[0, rows, :], w_ref[...], preferred_element_type=F32) + bi_ref[...]
        if c >= 1:
            short_conv(c - 1)
    short_conv(nb - 1)

    for b in range(nb):
        pb = p_ref[b * blk:(b + 1) * blk, :].astype(BF16)
        ph_ref[b] = jnp.dot(fwd_ref[...], pb, preferred_element_type=F32)
    for a in range(nb):
        _spectral_mix(g_ref, ph_ref, yh_ref, a, nb=nb, blk=blk, lag_blocks=nb - 1)
        y = jnp.dot(inv_ref[...], yh_ref[...], preferred_element_type=F32)
        rows = slice(a * blk, (a + 1) * blk)
        y = y + p_ref[rows, :] * skip_ref[...]
        o_ref[0, rows, :] = (y * x0_ref[rows, :]).astype(o_ref.dtype)


def _hy_mix(xn, w_in, b_in, w_short, b_short, gspec, skip, fwd, inv, *, tn, nb, blk):
    b, seq, d = xn.shape
    nt = d // tn

    def col(k):
        return lambda j, i: (0, k * nt + j)

    w_specs = [_resident((d, tn), col(k)) for k in range(3)]
    bi_specs = [pl.BlockSpec((1, tn), col(k)) for k in range(3)]
    ws_specs = [pl.BlockSpec((w_short.shape[0], tn), col(k)) for k in range(3)]
    bs_specs = [pl.BlockSpec((1, tn), col(k)) for k in range(3)]
    return pl.pallas_call(
        functools.partial(_hy_mix_kernel, nb=nb, blk=blk),
        grid=(nt, b),
        in_specs=[pl.BlockSpec((1, seq, d), lambda j, i: (i, 0, 0))]
        + w_specs + bi_specs + ws_specs + bs_specs
        + [_resident((2 * nb - 1, 2 * blk, tn), lambda j, i: (0, 0, j)),
           pl.BlockSpec((1, tn), lambda j, i: (0, j)),
           _resident(fwd.shape, lambda j, i: (0, 0)),
           _resident(inv.shape, lambda j, i: (0, 0))],
        out_specs=pl.BlockSpec((1, seq, tn), lambda j, i: (i, 0, j)),
        out_shape=jax.ShapeDtypeStruct((b, seq, d), BF16),
        scratch_shapes=[pltpu.VMEM((3, seq, tn), F32), pltpu.VMEM((seq, tn), F32), pltpu.VMEM((seq, tn), F32),
                        pltpu.VMEM((nb, 2 * blk, tn), F32), pltpu.VMEM((2 * blk, tn), BF16)],
        compiler_params=pltpu.CompilerParams(
            dimension_semantics=("arbitrary", "arbitrary"),
            vmem_limit_bytes=_vmem_limit(V7X_VMEM_BYTES)),
        name="hy_mix",
    )(xn, w_in, w_in, w_in, b_in, b_in, b_in, w_short, w_short, w_short, b_short, b_short, b_short,
      gspec, skip, fwd, inv)


def _filter_kernel(z_ref, w1_ref, b1_ref, f1_ref, w2_ref, b2_ref, f2_ref, w3_ref, b3_ref, f3_ref,
                   w4f_ref, w4b_ref, df_ref, fwd_ref, eneg_ref, g_ref, hf_ref, *, nb, blk):
    seq = z_ref.shape[0]

    @pl.when(pl.program_id(0) == 0)
    def _():
        hf = jnp.sin(f1_ref[...] * (jnp.dot(z_ref[...], w1_ref[...], preferred_element_type=F32) + b1_ref[...]))
        hf = jnp.sin(f2_ref[...] * (jnp.dot(hf, w2_ref[...], preferred_element_type=F32) + b2_ref[...]))
        hf_ref[...] = jnp.sin(f3_ref[...] * (jnp.dot(hf, w3_ref[...], preferred_element_type=F32) + b3_ref[...]))

    hf = hf_ref[...]
    row = lax.broadcasted_iota(jnp.int32, (seq, 1), 0)
    t = row.astype(F32) * (1.0 / (seq - 1))
    decay = jnp.exp(-t * df_ref[...])
    kf = jnp.dot(hf, w4f_ref[...], preferred_element_type=F32) * decay
    kb = jnp.dot(hf, w4b_ref[...], preferred_element_type=F32) * decay
    kb = jnp.where(row == 0, 0.0, kb)

    fwd = fwd_ref[...]
    eneg = eneg_ref[...]

    def spectra(k):
        pos, neg = [], []
        for jb in range(nb):
            kj = k[jb * blk:(jb + 1) * blk, :].astype(BF16)
            pos.append(jnp.dot(fwd, kj, preferred_element_type=F32))
            neg.append(jnp.dot(eneg, kj, preferred_element_type=F32) if jb < nb - 1 else None)
        return pos, neg

    fpos, fneg = spectra(kf)
    bpos, bneg = spectra(kb)

    def store(idx, val, conj_add=None):
        re, im = val[:blk], val[blk:]
        if conj_add is not None:
            re, im = re + conj_add[:blk], im - conj_add[blk:]
        g_ref[idx, 0:blk, :] = re
        g_ref[idx, blk:, :] = im

    store(nb - 1, fpos[0], conj_add=bpos[0])
    for c in range(1, nb):
        store(nb - 1 + c, fneg[c - 1] + fpos[c])
        gb = bneg[c - 1] + bpos[c]
        g_ref[nb - 1 - c, 0:blk, :] = gb[:blk]
        g_ref[nb - 1 - c, blk:, :] = -gb[blk:]


def _filter_spectra(z, w1, b1, f1, w2, b2, f2, w3, b3, f3, w4, deltas, fwd, eneg, *, td, nb, blk):
    seq = z.shape[0]
    d = deltas.shape[1]
    nt = d // td
    order = w2.shape[0]
    full = lambda a: _resident(a.shape, lambda j: (0,) * a.ndim)
    return pl.pallas_call(
        functools.partial(_filter_kernel, nb=nb, blk=blk),
        grid=(nt,),
        in_specs=[full(z), full(w1), full(b1), full(f1), full(w2), full(b2), full(f2), full(w3), full(b3), full(f3),
                  pl.BlockSpec((order, td), lambda j: (0, j)),
                  pl.BlockSpec((order, td), lambda j: (0, nt + j)),
                  pl.BlockSpec((1, td), lambda j: (0, j)),
                  full(fwd), full(eneg)],
        out_specs=pl.BlockSpec((2 * nb - 1, 2 * blk, td), lambda j: (0, 0, j)),
        out_shape=jax.ShapeDtypeStruct((2 * nb - 1, 2 * blk, d), F32),
        scratch_shapes=[pltpu.VMEM((seq, order), F32)],
        compiler_params=pltpu.CompilerParams(
            dimension_semantics=("arbitrary",),
            vmem_limit_bytes=_vmem_limit(2 * (2 * nb - 1) * 2 * blk * td * 4 + (4 * nb) * 2 * blk * td * 4
                                         + 6 * seq * td * 4 + (8 << 20))),
        name="hy_filter",
    )(z, w1, b1, f1, w2, b2, f2, w3, b3, f3, w4, w4, deltas, fwd, eneg)


def _proj_res_kernel(a_ref, x_ref, w_ref, b_ref, o_ref):
    o_ref[...] = x_ref[...] + jnp.dot(a_ref[...], w_ref[...], preferred_element_type=F32) + b_ref[...]


def _proj_res(a, x, w, b, *, tm):
    m, d = x.shape
    return pl.pallas_call(
        _proj_res_kernel,
        grid=(m // tm,),
        in_specs=[pl.BlockSpec((tm, d), lambda i: (i, 0)),
                  pl.BlockSpec((tm, d), lambda i: (i, 0)),
                  _resident((d, d), lambda i: (0, 0)),
                  _resident((1, d), lambda i: (0, 0))],
        out_specs=pl.BlockSpec((tm, d), lambda i: (i, 0)),
        out_shape=jax.ShapeDtypeStruct((m, d), F32),
        compiler_params=pltpu.CompilerParams(
            dimension_semantics=("arbitrary",),
            vmem_limit_bytes=_vmem_limit(d * d * 2 + 7 * tm * d * 4 + (4 << 20))),
        name="hy_out",
    )(a, x, w, b)


def _dft_angles(blk):
    f = (np.arange(blk, dtype=np.float64) + 0.5)[:, None]
    return f * (2.0 * np.pi / (2 * blk))


def _dft_tables(blk):
    w = _dft_angles(blk)
    q = np.arange(blk, dtype=np.float64)[None, :]
    ang = w * q
    fwd = np.concatenate([np.cos(ang), -np.sin(ang)], axis=0)
    ang_n = w * (q - blk)
    eneg = np.concatenate([np.cos(ang_n), -np.sin(ang_n)], axis=0)
    eneg[:, 0] = 0.0
    inv = np.concatenate([np.cos(ang).T, -np.sin(ang).T], axis=1) / blk
    return (jnp.asarray(fwd, dtype=BF16), jnp.asarray(eneg, dtype=BF16), jnp.asarray(inv, dtype=BF16))


def _dw_tables(blk, width, taps_pad):
    w = _dft_angles(blk)
    pad = (width - 1) // 2
    tab = np.zeros((3, 2 * blk, taps_pad), dtype=np.float64)
    for ci, c in enumerate((-1, 0, 1)):
        for j in range(width):
            e = pad - j - blk * c
            if abs(e) <= blk - 1:
                tab[ci, :blk, j] = np.cos(w[:, 0] * e)
                tab[ci, blk:, j] = -np.sin(w[:, 0] * e)
    return jnp.asarray(tab, dtype=F32)


def _hyena_positional_features(seq, pad_to):
    t = jnp.linspace(0.0, 1.0, seq, dtype=F32)[:, None]
    bands = jnp.linspace(1e-4, HYENA_N_BANDS - 1, HYENA_N_BANDS, dtype=F32)[None, :]
    wpos = (2.0 * math.pi) * jnp.arange(seq, dtype=F32)[:, None] / seq
    z = jnp.concatenate([t, jnp.cos(bands * wpos), -jnp.sin(bands * wpos)], axis=-1)
    return jnp.pad(z, ((0, 0), (0, pad_to - z.shape[1])))


def _hyena_decay_rates(d):
    max_decay = math.log(HYENA_DECAY_TARGET) / HYENA_FAST_DECAY_PCT
    min_decay = math.log(HYENA_DECAY_TARGET) / HYENA_SLOW_DECAY_PCT
    return jnp.abs(jnp.linspace(min_decay, max_decay, d, dtype=F32))[None, :]


def _pick(n, prefs):
    for p in prefs:
        if n % p == 0:
            return p
    return n


def kernel(x, norm_mix, norm_ffn, cv_w_pw1, cv_b_pw1, cv_w_dw, cv_b_dw, cv_ln_g, cv_ln_b, cv_w_pw2, cv_b_pw2, hy_w_in, hy_b_in, hy_w_short, hy_b_short, hy_f_w1, hy_f_b1, hy_f_freq1, hy_f_w2, hy_f_b2, hy_f_freq2, hy_f_w3, hy_f_b3, hy_f_freq3, hy_f_w4, hy_skip, hy_w_out, hy_b_out, ffn_w_gate, ffn_w_up, ffn_w_down, norm_final):
    bsz, seq, d = x.shape
    m = bsz * seq
    f = ffn_w_gate.shape[-1]
    depth = norm_mix.shape[0]
    assert depth == 2 and cv_w_pw1.shape[0] == 1 and hy_w_in.shape[0] == 1

    tm = _pick(m, (512, 256, 128))
    tf = _pick(f, (512, 256, 128))
    tn_col = _pick(d, (256, 128))
    td = _pick(d, (512, 256, 128))
    blk = _pick(seq, (CONV_BLOCK, 256, 128))
    nb = seq // blk
    dw_blk = _pick(seq, (DW_BLOCK, 128))
    width = cv_w_dw.shape[1]
    assert (width - 1) // 2 < dw_blk

    row = lambda a: a.reshape(1, -1).astype(F32)
    bf = lambda a: a.astype(BF16)

    h = x.reshape(m, d)

    taps_pad = -(-width // V7X_SUBLANES) * V7X_SUBLANES
    wdw = jnp.pad(cv_w_dw[0], ((0, taps_pad - width), (0, 0)))
    dw_fwd, _, dw_inv = _dft_tables(dw_blk)
    dw_spec = _dw_spectra(_dw_tables(dw_blk, width, taps_pad), wdw, td=td)
    y = _conf_in(x, row(norm_mix[0]), bf(cv_w_pw1[0]), row(cv_b_pw1[0]), dw_spec, row(cv_b_dw[0]),
                 dw_fwd, dw_inv, tn=tn_col, blk=dw_blk)
    h = _conf_out(y.reshape(m, d), h, row(cv_ln_g[0]), row(cv_ln_b[0]), bf(cv_w_pw2[0]), row(cv_b_pw2[0]), tm=tm)
    wg, wu, wd = bf(ffn_w_gate), bf(ffn_w_up), bf(ffn_w_down)
    h, hn = _ffn(h, row(norm_ffn[0]), wg, wu, wd, row(norm_mix[1]), layer=0, tm=tm, tf=tf, final_norm=False)

    fwd, eneg, inv = _dft_tables(blk)
    emb = hy_f_w1.shape[1]
    emb_pad = -(-emb // V7X_LANES) * V7X_LANES
    z = _hyena_positional_features(seq, emb_pad)
    w1 = jnp.pad(hy_f_w1[0], ((0, emb_pad - emb), (0, 0)))
    gspec = _filter_spectra(z, w1, row(hy_f_b1[0]), row(hy_f_freq1[0]), hy_f_w2[0], row(hy_f_b2[0]),
                            row(hy_f_freq2[0]), hy_f_w3[0], row(hy_f_b3[0]), row(hy_f_freq3[0]), hy_f_w4[0],
                            _hyena_decay_rates(d), fwd, eneg, td=td, nb=nb, blk=blk)
    gated = _hy_mix(hn.reshape(bsz, seq, d), bf(hy_w_in[0]), row(hy_b_in[0]), hy_w_short[0], row(hy_b_short[0]),
                    gspec, row(hy_skip[0]), fwd, inv, tn=tn_col, nb=nb, blk=blk)
    h = _proj_res(gated.reshape(m, d), h, bf(hy_w_out[0]), row(hy_b_out[0]), tm=tm)
    h = _ffn(h, row(norm_ffn[1]), wg, wu, wd, row(norm_final), layer=1, tm=tm, tf=tf, final_norm=True)
    return h.reshape(bsz, seq, d)
```

```python
import functools
import math

import jax
import jax.numpy as jnp
import numpy as np
from jax import lax
from jax.experimental import pallas as pl
from jax.experimental.pallas import tpu as pltpu

NORM_EPS = 1e-6
LN_EPS = 1e-5
HYENA_N_BANDS = 16
HYENA_FAST_DECAY_PCT = 0.3
HYENA_SLOW_DECAY_PCT = 1.5
HYENA_DECAY_TARGET = 1e-2

V7X_LANES = 128
V7X_SUBLANES = 8
V7X_BF16_SUBLANES = 16
V7X_VMEM_BYTES = 64 * 1024 * 1024

DW_BLOCK = 256
CONV_BLOCK = 512

F32 = jnp.float32
BF16 = jnp.bfloat16


def _vmem_limit(nbytes):
    return int(min(max(nbytes, 32 * 1024 * 1024), V7X_VMEM_BYTES - 4 * 1024 * 1024))


def _resident(block_shape, index_map):
    return pl.BlockSpec(block_shape, index_map, pipeline_mode=pl.Buffered(1))


def _rmsnorm_bf16(x, g):
    ms = jnp.mean(x * x, axis=-1, keepdims=True)
    return (x * lax.rsqrt(ms + NORM_EPS) * g).astype(BF16)


def _spectral_mix(g_ref, ph_ref, yh_ref, a, *, nb, blk, lag_blocks):
    lanes = yh_ref.shape[-1]
    slab = V7X_BF16_SUBLANES
    lane_w = min(lanes, 2 * V7X_LANES)
    for s in range(blk // slab):
        r_re = slice(s * slab, (s + 1) * slab)
        r_im = slice(blk + s * slab, blk + (s + 1) * slab)
        for lc in range(lanes // lane_w):
            ls = slice(lc * lane_w, (lc + 1) * lane_w)
            acc_re = acc_im = None
            for b in range(max(0, a - lag_blocks), min(nb, a + lag_blocks + 1)):
                gi = a - b + lag_blocks
                gre, gim = g_ref[gi, r_re, ls], g_ref[gi, r_im, ls]
                pre, pim = ph_ref[b, r_re, ls], ph_ref[b, r_im, ls]
                t_re = gre * pre - gim * pim
                t_im = gre * pim + gim * pre
                acc_re = t_re if acc_re is None else acc_re + t_re
                acc_im = t_im if acc_im is None else acc_im + t_im
            yh_ref[r_re, ls] = acc_re.astype(yh_ref.dtype)
            yh_ref[r_im, ls] = acc_im.astype(yh_ref.dtype)


def _dw_spectra_kernel(e_ref, w_ref, o_ref):
    for c in range(e_ref.shape[0]):
        o_ref[c] = jnp.dot(e_ref[c], w_ref[...], preferred_element_type=F32)


def _dw_spectra(etab, wdw, *, td):
    nc, rows, taps = etab.shape
    d = wdw.shape[1]
    return pl.pallas_call(
        _dw_spectra_kernel,
        grid=(d // td,),
        in_specs=[_resident(etab.shape, lambda j: (0, 0, 0)), pl.BlockSpec((taps, td), lambda j: (0, j))],
        out_specs=pl.BlockSpec((nc, rows, td), lambda j: (0, 0, j)),
        out_shape=jax.ShapeDtypeStruct((nc, rows, d), F32),
        compiler_params=pltpu.CompilerParams(dimension_semantics=("arbitrary",)),
        name="dw_spectra",
    )(etab, wdw)


def _conf_in_kernel(x_ref, g_ref, wa_ref, wb_ref, ba_ref, bb_ref, gs_ref, bdw_ref, fwd_ref, inv_ref,
                    y_ref, xn_ref, u_ref, ph_ref, yh_ref, *, nb, blk):
    @pl.when(pl.program_id(1) == 0)
    def _():
        xn_ref[...] = _rmsnorm_bf16(x_ref[0], g_ref[...])

    rc = min(CONV_BLOCK, u_ref.shape[0])
    for c in range(u_ref.shape[0] // rc):
        rows = slice(c * rc, (c + 1) * rc)
        a1 = jnp.dot(xn_ref[rows, :], wa_ref[...], preferred_element_type=F32) + ba_ref[...]
        a2 = jnp.dot(xn_ref[rows, :], wb_ref[...], preferred_element_type=F32) + bb_ref[...]
        u_ref[rows, :] = (a1 * jax.nn.sigmoid(a2)).astype(BF16)
    for b in range(nb):
        ph_ref[b] = jnp.dot(fwd_ref[...], u_ref[b * blk:(b + 1) * blk, :], preferred_element_type=F32)
    for a in range(nb):
        _spectral_mix(gs_ref, ph_ref, yh_ref, a, nb=nb, blk=blk, lag_blocks=1)
        y = jnp.dot(inv_ref[...], yh_ref[...], preferred_element_type=F32)
        y_ref[0, a * blk:(a + 1) * blk, :] = y + bdw_ref[...]


def _conf_in(x, g, w, b, gs, bdw, fwd, inv, *, tn, blk):
    bsz, seq, d = x.shape
    nt = d // tn
    nb = seq // blk
    return pl.pallas_call(
        functools.partial(_conf_in_kernel, nb=nb, blk=blk),
        grid=(bsz, nt),
        in_specs=[
            _resident((1, seq, d), lambda i, j: (i, 0, 0)),
            _resident((1, d), lambda i, j: (0, 0)),
            pl.BlockSpec((d, tn), lambda i, j: (0, j)),
            pl.BlockSpec((d, tn), lambda i, j: (0, nt + j)),
            pl.BlockSpec((1, tn), lambda i, j: (0, j)),
            pl.BlockSpec((1, tn), lambda i, j: (0, nt + j)),
            pl.BlockSpec((gs.shape[0], 2 * blk, tn), lambda i, j: (0, 0, j)),
            pl.BlockSpec((1, tn), lambda i, j: (0, j)),
            _resident(fwd.shape, lambda i, j: (0, 0)),
            _resident(inv.shape, lambda i, j: (0, 0)),
        ],
        out_specs=pl.BlockSpec((1, seq, tn), lambda i, j: (i, 0, j)),
        out_shape=jax.ShapeDtypeStruct((bsz, seq, d), F32),
        scratch_shapes=[pltpu.VMEM((seq, d), BF16), pltpu.VMEM((seq, tn), BF16),
                        pltpu.VMEM((nb, 2 * blk, tn), F32), pltpu.VMEM((2 * blk, tn), BF16)],
        compiler_params=pltpu.CompilerParams(
            dimension_semantics=("arbitrary", "arbitrary"),
            vmem_limit_bytes=_vmem_limit(seq * d * 6 + 8 * d * tn + 14 * seq * tn * 4 + (6 << 20))),
        name="conf_in",
    )(x, g, w, w, b, b, gs, bdw, fwd, inv)


def _conf_out_kernel(y_ref, x_ref, lg_ref, lb_ref, w2_ref, b2_ref, o_ref):
    y = y_ref[...]
    mu = jnp.mean(y, axis=-1, keepdims=True)
    yc = y - mu
    var = jnp.mean(yc * yc, axis=-1, keepdims=True)
    z = yc * lax.rsqrt(var + LN_EPS) * lg_ref[...] + lb_ref[...]
    act = (z * jax.nn.sigmoid(z)).astype(BF16)
    o_ref[...] = x_ref[...] + jnp.dot(act, w2_ref[...], preferred_element_type=F32) + b2_ref[...]


def _conf_out(y, x, lg, lb, w2, b2, *, tm):
    m, d = y.shape
    return pl.pallas_call(
        _conf_out_kernel,
        grid=(m // tm,),
        in_specs=[
            pl.BlockSpec((tm, d), lambda i: (i, 0)),
            pl.BlockSpec((tm, d), lambda i: (i, 0)),
            _resident((1, d), lambda i: (0, 0)),
            _resident((1, d), lambda i: (0, 0)),
            _resident((d, d), lambda i: (0, 0)),
            _resident((1, d), lambda i: (0, 0)),
        ],
        out_specs=pl.BlockSpec((tm, d), lambda i: (i, 0)),
        out_shape=jax.ShapeDtypeStruct((m, d), F32),
        compiler_params=pltpu.CompilerParams(
            dimension_semantics=("arbitrary",),
            vmem_limit_bytes=_vmem_limit(d * d * 2 + 12 * tm * d * 4 + (4 << 20))),
        name="conf_out",
    )(y, x, lg, lb, w2, b2)


def _ffn_kernel(x_ref, g_ref, wg_ref, wu_ref, wd_ref, gn_ref, o_ref, *rest, final_norm):
    xn_ref = rest[-1]
    j = pl.program_id(1)

    @pl.when(j == 0)
    def _():
        x = x_ref[...]
        xn_ref[...] = _rmsnorm_bf16(x, g_ref[...])
        o_ref[...] = x

    xn = xn_ref[...]
    gate = jnp.dot(xn, wg_ref[...], preferred_element_type=F32)
    up = jnp.dot(xn, wu_ref[...], preferred_element_type=F32)
    act = (gate * jax.nn.sigmoid(gate) * up).astype(BF16)
    o_ref[...] += jnp.dot(act, wd_ref[...], preferred_element_type=F32)

    @pl.when(j == pl.num_programs(1) - 1)
    def _():
        h = o_ref[...]
        ms = jnp.mean(h * h, axis=-1, keepdims=True)
        hn = h * lax.rsqrt(ms + NORM_EPS) * gn_ref[...]
        if final_norm:
            o_ref[...] = hn
        else:
            rest[0][...] = hn.astype(BF16)


def _ffn(x, g, wg, wu, wd, gn, *, layer, tm, tf, final_norm):
    m, d = x.shape
    f = wg.shape[-1]
    row_spec = pl.BlockSpec((tm, d), lambda i, j: (i, 0))
    out_specs, out_shape = row_spec, jax.ShapeDtypeStruct((m, d), F32)
    if not final_norm:
        out_specs = [row_spec, row_spec]
        out_shape = [out_shape, jax.ShapeDtypeStruct((m, d), BF16)]
    return pl.pallas_call(
        functools.partial(_ffn_kernel, final_norm=final_norm),
        grid=(m // tm, f // tf),
        in_specs=[
            row_spec,
            _resident((1, d), lambda i, j: (0, 0)),
            pl.BlockSpec((None, d, tf), lambda i, j: (layer, 0, j)),
            pl.BlockSpec((None, d, tf), lambda i, j: (layer, 0, j)),
            pl.BlockSpec((None, tf, d), lambda i, j: (layer, j, 0)),
            _resident((1, d), lambda i, j: (0, 0)),
        ],
        out_specs=out_specs,
        out_shape=out_shape,
        scratch_shapes=[pltpu.VMEM((tm, d), BF16)],
        compiler_params=pltpu.CompilerParams(
            dimension_semantics=("arbitrary", "arbitrary"),
            vmem_limit_bytes=_vmem_limit(5 * tm * d * 4 + 3 * tm * d * 2 + 12 * d * tf + 5 * tm * tf * 4 + (4 << 20))),
        name="ffn_final" if final_norm else "ffn",
    )(x, g, wg, wu, wd, gn)


def _hy_mix_kernel(xn_ref, w0_ref, w1_ref, w2_ref, bi0_ref, bi1_ref, bi2_ref,
                   ws0_ref, ws1_ref, ws2_ref, bs0_ref, bs1_ref, bs2_ref,
                   g_ref, skip_ref, fwd_ref, inv_ref, o_ref, z_ref, p_ref, x0_ref, ph_ref, yh_ref, *, nb, blk):
    seq = xn_ref.shape[1]
    halo = V7X_SUBLANES
    branches = ((w0_ref, bi0_ref, ws0_ref, bs0_ref), (w1_ref, bi1_ref, ws1_ref, bs1_ref),
                (w2_ref, bi2_ref, ws2_ref, bs2_ref))

    def short_conv(k):
        lo, hi = max(k * blk - halo, 0), min((k + 1) * blk + halo, seq)
        n = hi - lo
        off = k * blk - lo
        row = lax.broadcasted_iota(jnp.int32, (n, 1), 0)
        outs = []
        for br, (_, _, ws_ref, bs_ref) in enumerate(branches):
            z = z_ref[br, lo:hi, :]
            zp = pltpu.roll(z, 1, axis=0)
            zn = pltpu.roll(z, n - 1, axis=0)
            if lo == 0:
                zp = jnp.where(row == 0, 0.0, zp)
            if hi == seq:
                zn = jnp.where(row == n - 1, 0.0, zn)
            y = ws_ref[0:1, :] * zp + ws_ref[1:2, :] * z + ws_ref[2:3, :] * zn + bs_ref[...]
            outs.append(y[off:off + blk, :])
        x0_ref[k * blk:(k + 1) * blk, :] = outs[0]
        p_ref[k * blk:(k + 1) * blk, :] = outs[2] * outs[1]

    for c in range(nb):
        rows = slice(c * blk, (c + 1) * blk)
        for br, (w_ref, bi_ref, _, _) in enumerate(branches):
            z_ref[br, rows, :] = jnp.dot(xn_ref[0, rows, :], w_ref[...], preferred_element_type=F32) + bi_ref[...]
        if c >= 1:
            short_conv(c - 1)
    short_conv(nb - 1)

    for b in range(nb):
        pb = p_ref[b * blk:(b + 1) * blk, :].astype(BF16)
        ph_ref[b] = jnp.dot(fwd_ref[...], pb, preferred_element_type=F32)
    for a in range(nb):
        _spectral_mix(g_ref, ph_ref, yh_ref, a, nb=nb, blk=blk, lag_blocks=nb - 1)
        y = jnp.dot(inv_ref[...], yh_ref[...], preferred_element_type=F32)
        rows = slice(a * blk, (a + 1) * blk)
        y = y + p_ref[rows, :] * skip_ref[...]
        o_ref[0, rows, :] = (y * x0_ref[rows, :]).astype(o_ref.dtype)


def _hy_mix(xn, w_in, b_in, w_short, b_short, gspec, skip, fwd, inv, *, tn, nb, blk):
    b, seq, d = xn.shape
    nt = d // tn

    def col(k):
        return lambda j, i: (0, k * nt + j)

    w_specs = [_resident((d, tn), col(k)) for k in range(3)]
    bi_specs = [pl.BlockSpec((1, tn), col(k)) for k in range(3)]
    ws_specs = [pl.BlockSpec((w_short.shape[0], tn), col(k)) for k in range(3)]
    bs_specs = [pl.BlockSpec((1, tn), col(k)) for k in range(3)]
    return pl.pallas_call(
        functools.partial(_hy_mix_kernel, nb=nb, blk=blk),
        grid=(nt, b),
        in_specs=[pl.BlockSpec((1, seq, d), lambda j, i: (i, 0, 0))]
        + w_specs + bi_specs + ws_specs + bs_specs
        + [_resident((2 * nb - 1, 2 * blk, tn), lambda j, i: (0, 0, j)),
           pl.BlockSpec((1, tn), lambda j, i: (0, j)),
           _resident(fwd.shape, lambda j, i: (0, 0)),
           _resident(inv.shape, lambda j, i: (0, 0))],
        out_specs=pl.BlockSpec((1, seq, tn), lambda j, i: (i, 0, j)),
        out_shape=jax.ShapeDtypeStruct((b, seq, d), BF16),
        scratch_shapes=[pltpu.VMEM((3, seq, tn), F32), pltpu.VMEM((seq, tn), F32), pltpu.VMEM((seq, tn), F32),
                        pltpu.VMEM((nb, 2 * blk, tn), F32), pltpu.VMEM((2 * blk, tn), BF16)],
        compiler_params=pltpu.CompilerParams(
            dimension_semantics=("arbitrary", "arbitrary"),
            vmem_limit_bytes=_vmem_limit(V7X_VMEM_BYTES)),
        name="hy_mix",
    )(xn, w_in, w_in, w_in, b_in, b_in, b_in, w_short, w_short, w_short, b_short, b_short, b_short,
      gspec, skip, fwd, inv)


def _filter_kernel(z_ref, w1_ref, b1_ref, f1_ref, w2_ref, b2_ref, f2_ref, w3_ref, b3_ref, f3_ref,
                   w4f_ref, w4b_ref, df_ref, fwd_ref, eneg_ref, g_ref, hf_ref, *, nb, blk):
    seq = z_ref.shape[0]

    @pl.when(pl.program_id(0) == 0)
    def _():
        hf = jnp.sin(f1_ref[...] * (jnp.dot(z_ref[...], w1_ref[...], preferred_element_type=F32) + b1_ref[...]))
        hf = jnp.sin(f2_ref[...] * (jnp.dot(hf, w2_ref[...], preferred_element_type=F32) + b2_ref[...]))
        hf_ref[...] = jnp.sin(f3_ref[...] * (jnp.dot(hf, w3_ref[...], preferred_element_type=F32) + b3_ref[...]))

    hf = hf_ref[...]
    row = lax.broadcasted_iota(jnp.int32, (seq, 1), 0)
    t = row.astype(F32) * (1.0 / (seq - 1))
    decay = jnp.exp(-t * df_ref[...])
    kf = jnp.dot(hf, w4f_ref[...], preferred_element_type=F32) * decay
    kb = jnp.dot(hf, w4b_ref[...], preferred_element_type=F32) * decay
    kb = jnp.where(row == 0, 0.0, kb)

    fwd = fwd_ref[...]
    eneg = eneg_ref[...]

    def spectra(k):
        pos, neg = [], []
        for jb in range(nb):
            kj = k[jb * blk:(jb + 1) * blk, :].astype(BF16)
            pos.append(jnp.dot(fwd, kj, preferred_element_type=F32))
            neg.append(jnp.dot(eneg, kj, preferred_element_type=F32) if jb < nb - 1 else None)
        return pos, neg

    fpos, fneg = spectra(kf)
    bpos, bneg = spectra(kb)

    def store(idx, val, conj_add=None):
        re, im = val[:blk], val[blk:]
        if conj_add is not None:
            re, im = re + conj_add[:blk], im - conj_add[blk:]
        g_ref[idx, 0:blk, :] = re
        g_ref[idx, blk:, :] = im

    store(nb - 1, fpos[0], conj_add=bpos[0])
    for c in range(1, nb):
        store(nb - 1 + c, fneg[c - 1] + fpos[c])
        gb = bneg[c - 1] + bpos[c]
        g_ref[nb - 1 - c, 0:blk, :] = gb[:blk]
        g_ref[nb - 1 - c, blk:, :] = -gb[blk:]


def _filter_spectra(z, w1, b1, f1, w2, b2, f2, w3, b3, f3, w4, deltas, fwd, eneg, *, td, nb, blk):
    seq = z.shape[0]
    d = deltas.shape[1]
    nt = d // td
    order = w2.shape[0]
    full = lambda a: _resident(a.shape, lambda j: (0,) * a.ndim)
    return pl.pallas_call(
        functools.partial(_filter_kernel, nb=nb, blk=blk),
        grid=(nt,),
        in_specs=[full(z), full(w1), full(b1), full(f1), full(w2), full(b2), full(f2), full(w3), full(b3), full(f3),
                  pl.BlockSpec((order, td), lambda j: (0, j)),
                  pl.BlockSpec((order, td), lambda j: (0, nt + j)),
                  pl.BlockSpec((1, td), lambda j: (0, j)),
                  full(fwd), full(eneg)],
        out_specs=pl.BlockSpec((2 * nb - 1, 2 * blk, td), lambda j: (0, 0, j)),
        out_shape=jax.ShapeDtypeStruct((2 * nb - 1, 2 * blk, d), F32),
        scratch_shapes=[pltpu.VMEM((seq, order), F32)],
        compiler_params=pltpu.CompilerParams(
            dimension_semantics=("arbitrary",),
            vmem_limit_bytes=_vmem_limit(2 * (2 * nb - 1) * 2 * blk * td * 4 + (4 * nb) * 2 * blk * td * 4
                                         + 6 * seq * td * 4 + (8 << 20))),
        name="hy_filter",
    )(z, w1, b1, f1, w2, b2, f2, w3, b3, f3, w4, w4, deltas, fwd, eneg)


def _proj_res_kernel(a_ref, x_ref, w_ref, b_ref, o_ref):
    o_ref[...] = x_ref[...] + jnp.dot(a_ref[...], w_ref[...], preferred_element_type=F32) + b_ref[...]


def _proj_res(a, x, w, b, *, tm):
    m, d = x.shape
    return pl.pallas_call(
        _proj_res_kernel,
        grid=(m // tm,),
        in_specs=[pl.BlockSpec((tm, d), lambda i: (i, 0)),
                  pl.BlockSpec((tm, d), lambda i: (i, 0)),
                  _resident((d, d), lambda i: (0, 0)),
                  _resident((1, d), lambda i: (0, 0))],
        out_specs=pl.BlockSpec((tm, d), lambda i: (i, 0)),
        out_shape=jax.ShapeDtypeStruct((m, d), F32),
        compiler_params=pltpu.CompilerParams(
            dimension_semantics=("arbitrary",),
            vmem_limit_bytes=_vmem_limit(d * d * 2 + 7 * tm * d * 4 + (4 << 20))),
        name="hy_out",
    )(a, x, w, b)


def _dft_angles(blk):
    f = (np.arange(blk, dtype=np.float64) + 0.5)[:, None]
    return f * (2.0 * np.pi / (2 * blk))


def _dft_tables(blk):
    w = _dft_angles(blk)
    q = np.arange(blk, dtype=np.float64)[None, :]
    ang = w * q
    fwd = np.concatenate([np.cos(ang), -np.sin(ang)], axis=0)
    ang_n = w * (q - blk)
    eneg = np.concatenate([np.cos(ang_n), -np.sin(ang_n)], axis=0)
    eneg[:, 0] = 0.0
    inv = np.concatenate([np.cos(ang).T, -np.sin(ang).T], axis=1) / blk
    return (jnp.asarray(fwd, dtype=BF16), jnp.asarray(eneg, dtype=BF16), jnp.asarray(inv, dtype=BF16))


def _dw_tables(blk, width, taps_pad):
    w = _dft_angles(blk)
    pad = (width - 1) // 2
    tab = np.zeros((3, 2 * blk, taps_pad), dtype=np.float64)
    for ci, c in enumerate((-1, 0, 1)):
        for j in range(width):
            e = pad - j - blk * c
            if abs(e) <= blk - 1:
                tab[ci, :blk, j] = np.cos(w[:, 0] * e)
                tab[ci, blk:, j] = -np.sin(w[:, 0] * e)
    return jnp.asarray(tab, dtype=F32)


def _hyena_positional_features(seq, pad_to):
    t = jnp.linspace(0.0, 1.0, seq, dtype=F32)[:, None]
    bands = jnp.linspace(1e-4, HYENA_N_BANDS - 1, HYENA_N_BANDS, dtype=F32)[None, :]
    wpos = (2.0 * math.pi) * jnp.arange(seq, dtype=F32)[:, None] / seq
    z = jnp.concatenate([t, jnp.cos(bands * wpos), -jnp.sin(bands * wpos)], axis=-1)
    return jnp.pad(z, ((0, 0), (0, pad_to - z.shape[1])))


def _hyena_decay_rates(d):
    max_decay = math.log(HYENA_DECAY_TARGET) / HYENA_FAST_DECAY_PCT
    min_decay = math.log(HYENA_DECAY_TARGET) / HYENA_SLOW_DECAY_PCT
    return jnp.abs(jnp.linspace(min_decay, max_decay, d, dtype=F32))[None, :]


def _pick(n, prefs):
    for p in prefs:
        if n % p == 0:
            return p
    return n


def kernel(x, norm_mix, norm_ffn, cv_w_pw1, cv_b_pw1, cv_w_dw, cv_b_dw, cv_ln_g, cv_ln_b, cv_w_pw2, cv_b_pw2, hy_w_in, hy_b_in, hy_w_short, hy_b_short, hy_f_w1, hy_f_b1, hy_f_freq1, hy_f_w2, hy_f_b2, hy_f_freq2, hy_f_w3, hy_f_b3, hy_f_freq3, hy_f_w4, hy_skip, hy_w_out, hy_b_out, ffn_w_gate, ffn_w_up, ffn_w_down, norm_final):
    bsz, seq, d = x.shape
    m = bsz * seq
    f = ffn_w_gate.shape[-1]
    depth = norm_mix.shape[0]
    assert depth == 2 and cv_w_pw1.shape[0] == 1 and hy_w_in.shape[0] == 1

    tm = _pick(m, (512, 256, 128))
    tf = _pick(f, (512, 256, 128))
    tn_col = _pick(d, (256, 128))
    td = _pick(d, (512, 256, 128))
    blk = _pick(seq, (CONV_BLOCK, 256, 128))
    nb = seq // blk
    dw_blk = _pick(seq, (DW_BLOCK, 128))
    width = cv_w_dw.shape[1]
    assert (width - 1) // 2 < dw_blk

    row = lambda a: a.reshape(1, -1).astype(F32)
    bf = lambda a: a.astype(BF16)

    h = x.reshape(m, d)

    taps_pad = -(-width // V7X_SUBLANES) * V7X_SUBLANES
    wdw = jnp.pad(cv_w_dw[0], ((0, taps_pad - width), (0, 0)))
    dw_fwd, _, dw_inv = _dft_tables(dw_blk)
    dw_spec = _dw_spectra(_dw_tables(dw_blk, width, taps_pad), wdw, td=td)
    y = _conf_in(x, row(norm_mix[0]), bf(cv_w_pw1[0]), row(cv_b_pw1[0]), dw_spec, row(cv_b_dw[0]),
                 dw_fwd, dw_inv, tn=tn_col, blk=dw_blk)
    h = _conf_out(y.reshape(m, d), h, row(cv_ln_g[0]), row(cv_ln_b[0]), bf(cv_w_pw2[0]), row(cv_b_pw2[0]), tm=tm)
    wg, wu, wd = bf(ffn_w_gate), bf(ffn_w_up), bf(ffn_w_down)
    h, hn = _ffn(h, row(norm_ffn[0]), wg, wu, wd, row(norm_mix[1]), layer=0, tm=tm, tf=tf, final_norm=False)

    fwd, eneg, inv = _dft_tables(blk)
    emb = hy_f_w1.shape[1]
    emb_pad = -(-emb // V7X_LANES) * V7X_LANES
    z = _hyena_positional_features(seq, emb_pad)
    w1 = jnp.pad(hy_f_w1[0], ((0, emb_pad - emb), (0, 0)))
    gspec = _filter_spectra(z, w1, row(hy_f_b1[0]), row(hy_f_freq1[0]), hy_f_w2[0], row(hy_f_b2[0]),
                            row(hy_f_freq2[0]), hy_f_w3[0], row(hy_f_b3[0]), row(hy_f_freq3[0]), hy_f_w4[0],
                            _hyena_decay_rates(d), fwd, eneg, td=td, nb=nb, blk=blk)
    gated = _hy_mix(hn.reshape(bsz, seq, d), bf(hy_w_in[0]), row(hy_b_in[0]), hy_w_short[0], row(hy_b_short[0]),
                    gspec, row(hy_skip[0]), fwd, inv, tn=tn_col, nb=nb, blk=blk)
    h = _proj_res(gated.reshape(m, d), h, bf(hy_w_out[0]), row(hy_b_out[0]), tm=tm)
    h = _ffn(h, row(norm_ffn[1]), wg, wu, wd, row(norm_final), layer=1, tm=tm, tf=tf, final_norm=True)
    return h.reshape(bsz, seq, d)
```

```python
import functools
import math

import jax
import jax.numpy as jnp
import numpy as np
from jax import lax
from jax.experimental import pallas as pl
from jax.experimental.pallas import tpu as pltpu

NORM_EPS = 1e-6
LN_EPS = 1e-5
HYENA_N_BANDS = 16
HYENA_FAST_DECAY_PCT = 0.3
HYENA_SLOW_DECAY_PCT = 1.5
HYENA_DECAY_TARGET = 1e-2

V7X_LANES = 128
V7X_SUBLANES = 8
V7X_BF16_SUBLANES = 16
V7X_VMEM_BYTES = 64 * 1024 * 1024

DW_BLOCK = 256
CONV_BLOCK = 512

F32 = jnp.float32
BF16 = jnp.bfloat16


def _vmem_limit(nbytes):
    return int(min(max(nbytes, 32 * 1024 * 1024), V7X_VMEM_BYTES - 4 * 1024 * 1024))


def _resident(block_shape, index_map):
    return pl.BlockSpec(block_shape, index_map, pipeline_mode=pl.Buffered(1))


def _rmsnorm_bf16(x, g):
    ms = jnp.mean(x * x, axis=-1, keepdims=True)
    return (x * lax.rsqrt(ms + NORM_EPS) * g).astype(BF16)


def _spectral_mix(g_ref, ph_ref, yh_ref, a, *, nb, blk, lag_blocks):
    lanes = yh_ref.shape[-1]
    slab = V7X_BF16_SUBLANES
    lane_w = min(lanes, 2 * V7X_LANES)
    for s in range(blk // slab):
        r_re = slice(s * slab, (s + 1) * slab)
        r_im = slice(blk + s * slab, blk + (s + 1) * slab)
        for lc in range(lanes // lane_w):
            ls = slice(lc * lane_w, (lc + 1) * lane_w)
            acc_re = acc_im = None
            for b in range(max(0, a - lag_blocks), min(nb, a + lag_blocks + 1)):
                gi = a - b + lag_blocks
                gre, gim = g_ref[gi, r_re, ls], g_ref[gi, r_im, ls]
                pre, pim = ph_ref[b, r_re, ls], ph_ref[b, r_im, ls]
                t_re = gre * pre - gim * pim
                t_im = gre * pim + gim * pre
                acc_re = t_re if acc_re is None else acc_re + t_re
                acc_im = t_im if acc_im is None else acc_im + t_im
            yh_ref[r_re, ls] = acc_re.astype(yh_ref.dtype)
            yh_ref[r_im, ls] = acc_im.astype(yh_ref.dtype)


def _spectral_mix_dense(g_ref, ph_ref, yh_ref, *, nb, blk):
    lanes = yh_ref.shape[-1]
    slab = V7X_BF16_SUBLANES
    lane_w = min(lanes, 2 * V7X_LANES)
    pairs = [list(range(a0, min(a0 + 2, nb))) for a0 in range(0, nb, 2)]
    for s in range(blk // slab):
        r_re = slice(s * slab, (s + 1) * slab)
        r_im = slice(blk + s * slab, blk + (s + 1) * slab)
        for lc in range(lanes // lane_w):
            ls = slice(lc * lane_w, (lc + 1) * lane_w)
            for pair in pairs:
                acc = {}
                for b in range(nb):
                    pre, pim = ph_ref[b, r_re, ls], ph_ref[b, r_im, ls]
                    for a in pair:
                        gi = a - b + nb - 1
                        gre, gim = g_ref[gi, r_re, ls], g_ref[gi, r_im, ls]
                        t_re = gre * pre - gim * pim
                        t_im = gre * pim + gim * pre
                        acc[a] = (t_re, t_im) if a not in acc else (acc[a][0] + t_re, acc[a][1] + t_im)
                for a in pair:
                    yh_ref[a, r_re, ls] = acc[a][0].astype(yh_ref.dtype)
                    yh_ref[a, r_im, ls] = acc[a][1].astype(yh_ref.dtype)


def _dw_spectra_kernel(e_ref, w_ref, o_ref):
    for c in range(e_ref.shape[0]):
        o_ref[c] = jnp.dot(e_ref[c], w_ref[...], preferred_element_type=F32)


def _dw_spectra(etab, wdw, *, td):
    nc, rows, taps = etab.shape
    d = wdw.shape[1]
    return pl.pallas_call(
        _dw_spectra_kernel,
        grid=(d // td,),
        in_specs=[_resident(etab.shape, lambda j: (0, 0, 0)), pl.BlockSpec((taps, td), lambda j: (0, j))],
        out_specs=pl.BlockSpec((nc, rows, td), lambda j: (0, 0, j)),
        out_shape=jax.ShapeDtypeStruct((nc, rows, d), F32),
        compiler_params=pltpu.CompilerParams(dimension_semantics=("arbitrary",)),
        name="dw_spectra",
    )(etab, wdw)


def _conf_in_kernel(x_ref, g_ref, wa_ref, wb_ref, ba_ref, bb_ref, gs_ref, bdw_ref, fwd_ref, inv_ref,
                    y_ref, xn_ref, u_ref, ph_ref, yh_ref, *, nb, blk):
    @pl.when(pl.program_id(1) == 0)
    def _():
        xn_ref[...] = _rmsnorm_bf16(x_ref[0], g_ref[...])

    rc = min(CONV_BLOCK, u_ref.shape[0])
    for c in range(u_ref.shape[0] // rc):
        rows = slice(c * rc, (c + 1) * rc)
        a1 = jnp.dot(xn_ref[rows, :], wa_ref[...], preferred_element_type=F32) + ba_ref[...]
        a2 = jnp.dot(xn_ref[rows, :], wb_ref[...], preferred_element_type=F32) + bb_ref[...]
        u_ref[rows, :] = (a1 * jax.nn.sigmoid(a2)).astype(BF16)
    for b in range(nb):
        ph_ref[b] = jnp.dot(fwd_ref[...], u_ref[b * blk:(b + 1) * blk, :], preferred_element_type=F32)
    for a in range(nb):
        _spectral_mix(gs_ref, ph_ref, yh_ref, a, nb=nb, blk=blk, lag_blocks=1)
        y = jnp.dot(inv_ref[...], yh_ref[...], preferred_element_type=F32)
        y_ref[0, a * blk:(a + 1) * blk, :] = y + bdw_ref[...]


def _conf_in(x, g, w, b, gs, bdw, fwd, inv, *, tn, blk):
    bsz, seq, d = x.shape
    nt = d // tn
    nb = seq // blk
    return pl.pallas_call(
        functools.partial(_conf_in_kernel, nb=nb, blk=blk),
        grid=(bsz, nt),
        in_specs=[
            _resident((1, seq, d), lambda i, j: (i, 0, 0)),
            _resident((1, d), lambda i, j: (0, 0)),
            pl.BlockSpec((d, tn), lambda i, j: (0, j)),
            pl.BlockSpec((d, tn), lambda i, j: (0, nt + j)),
            pl.BlockSpec((1, tn), lambda i, j: (0, j)),
            pl.BlockSpec((1, tn), lambda i, j: (0, nt + j)),
            pl.BlockSpec((gs.shape[0], 2 * blk, tn), lambda i, j: (0, 0, j)),
            pl.BlockSpec((1, tn), lambda i, j: (0, j)),
            _resident(fwd.shape, lambda i, j: (0, 0)),
            _resident(inv.shape, lambda i, j: (0, 0)),
        ],
        out_specs=pl.BlockSpec((1, seq, tn), lambda i, j: (i, 0, j)),
        out_shape=jax.ShapeDtypeStruct((bsz, seq, d), F32),
        scratch_shapes=[pltpu.VMEM((seq, d), BF16), pltpu.VMEM((seq, tn), BF16),
                        pltpu.VMEM((nb, 2 * blk, tn), F32), pltpu.VMEM((2 * blk, tn), BF16)],
        compiler_params=pltpu.CompilerParams(
            dimension_semantics=("arbitrary", "arbitrary"),
            vmem_limit_bytes=_vmem_limit(seq * d * 6 + 8 * d * tn + 14 * seq * tn * 4 + (6 << 20))),
        name="conf_in",
    )(x, g, w, w, b, b, gs, bdw, fwd, inv)


def _conf_out_kernel(y_ref, x_ref, lg_ref, lb_ref, w2_ref, b2_ref, o_ref):
    y = y_ref[...]
    mu = jnp.mean(y, axis=-1, keepdims=True)
    yc = y - mu
    var = jnp.mean(yc * yc, axis=-1, keepdims=True)
    z = yc * lax.rsqrt(var + LN_EPS) * lg_ref[...] + lb_ref[...]
    act = (z * jax.nn.sigmoid(z)).astype(BF16)
    o_ref[...] = x_ref[...] + jnp.dot(act, w2_ref[...], preferred_element_type=F32) + b2_ref[...]


def _conf_out(y, x, lg, lb, w2, b2, *, tm):
    m, d = y.shape
    return pl.pallas_call(
        _conf_out_kernel,
        grid=(m // tm,),
        in_specs=[
            pl.BlockSpec((tm, d), lambda i: (i, 0)),
            pl.BlockSpec((tm, d), lambda i: (i, 0)),
            _resident((1, d), lambda i: (0, 0)),
            _resident((1, d), lambda i: (0, 0)),
            _resident((d, d), lambda i: (0, 0)),
            _resident((1, d), lambda i: (0, 0)),
        ],
        out_specs=pl.BlockSpec((tm, d), lambda i: (i, 0)),
        out_shape=jax.ShapeDtypeStruct((m, d), F32),
        compiler_params=pltpu.CompilerParams(
            dimension_semantics=("arbitrary",),
            vmem_limit_bytes=_vmem_limit(d * d * 2 + 12 * tm * d * 4 + (4 << 20))),
        name="conf_out",
    )(y, x, lg, lb, w2, b2)


def _ffn_kernel(x_ref, g_ref, wg_ref, wu_ref, wd_ref, gn_ref, o_ref, *rest, final_norm):
    xn_ref = rest[-1]
    j = pl.program_id(1)

    @pl.when(j == 0)
    def _():
        x = x_ref[...]
        xn_ref[...] = _rmsnorm_bf16(x, g_ref[...])
        o_ref[...] = x

    xn = xn_ref[...]
    gate = jnp.dot(xn, wg_ref[...], preferred_element_type=F32)
    up = jnp.dot(xn, wu_ref[...], preferred_element_type=F32)
    act = (gate * jax.nn.sigmoid(gate) * up).astype(BF16)
    o_ref[...] += jnp.dot(act, wd_ref[...], preferred_element_type=F32)

    @pl.when(j == pl.num_programs(1) - 1)
    def _():
        h = o_ref[...]
        ms = jnp.mean(h * h, axis=-1, keepdims=True)
        hn = h * lax.rsqrt(ms + NORM_EPS) * gn_ref[...]
        if final_norm:
            o_ref[...] = hn
        else:
            rest[0][...] = hn.astype(BF16)


def _ffn(x, g, wg, wu, wd, gn, *, layer, tm, tf, final_norm):
    m, d = x.shape
    f = wg.shape[-1]
    row_spec = pl.BlockSpec((tm, d), lambda i, j: (i, 0))
    out_specs, out_shape = row_spec, jax.ShapeDtypeStruct((m, d), F32)
    if not final_norm:
        out_specs = [row_spec, row_spec]
        out_shape = [out_shape, jax.ShapeDtypeStruct((m, d), BF16)]
    return pl.pallas_call(
        functools.partial(_ffn_kernel, final_norm=final_norm),
        grid=(m // tm, f // tf),
        in_specs=[
            row_spec,
            _resident((1, d), lambda i, j: (0, 0)),
            pl.BlockSpec((None, d, tf), lambda i, j: (layer, 0, j)),
            pl.BlockSpec((None, d, tf), lambda i, j: (layer, 0, j)),
            pl.BlockSpec((None, tf, d), lambda i, j: (layer, j, 0)),
            _resident((1, d), lambda i, j: (0, 0)),
        ],
        out_specs=out_specs,
        out_shape=out_shape,
        scratch_shapes=[pltpu.VMEM((tm, d), BF16)],
        compiler_params=pltpu.CompilerParams(
            dimension_semantics=("arbitrary", "arbitrary"),
            vmem_limit_bytes=_vmem_limit(5 * tm * d * 4 + 3 * tm * d * 2 + 12 * d * tf + 5 * tm * tf * 4 + (4 << 20))),
        name="ffn_final" if final_norm else "ffn",
    )(x, g, wg, wu, wd, gn)


def _hy_mix_kernel(xn_ref, w0_ref, w1_ref, w2_ref, bi0_ref, bi1_ref, bi2_ref,
                   ws0_ref, ws1_ref, ws2_ref, bs0_ref, bs1_ref, bs2_ref,
                   g_ref, skip_ref, fwd_ref, inv_ref, o_ref, z_ref, p_ref, x0_ref, ph_ref, yh_ref, *, nb, blk):
    seq = xn_ref.shape[1]
    halo = V7X_SUBLANES
    branches = ((w0_ref, bi0_ref, ws0_ref, bs0_ref), (w1_ref, bi1_ref, ws1_ref, bs1_ref),
                (w2_ref, bi2_ref, ws2_ref, bs2_ref))

    def short_conv(k):
        lo, hi = max(k * blk - halo, 0), min((k + 1) * blk + halo, seq)
        n = hi - lo
        off = k * blk - lo
        row = lax.broadcasted_iota(jnp.int32, (n, 1), 0)
        outs = []
        for br, (_, _, ws_ref, bs_ref) in enumerate(branches):
            z = z_ref[br, lo:hi, :]
            zp = pltpu.roll(z, 1, axis=0)
            zn = pltpu.roll(z, n - 1, axis=0)
            if lo == 0:
                zp = jnp.where(row == 0, 0.0, zp)
            if hi == seq:
                zn = jnp.where(row == n - 1, 0.0, zn)
            y = ws_ref[0:1, :] * zp + ws_ref[1:2, :] * z + ws_ref[2:3, :] * zn + bs_ref[...]
            outs.append(y[off:off + blk, :])
        x0_ref[k * blk:(k + 1) * blk, :] = outs[0]
        p_ref[k * blk:(k + 1) * blk, :] = outs[2] * outs[1]

    for c in range(nb):
        rows = slice(c * blk, (c + 1) * blk)
        for br, (w_ref, bi_ref, _, _) in enumerate(branches):
            z_ref[br, rows, :] = jnp.dot(xn_ref[0, rows, :], w_ref[...], preferred_element_type=F32) + bi_ref[...]
        if c >= 1:
            short_conv(c - 1)
    short_conv(nb - 1)

    for b in range(nb):
        pb = p_ref[b * blk:(b + 1) * blk, :].astype(BF16)
        ph_ref[b] = jnp.dot(fwd_ref[...], pb, preferred_element_type=F32)
    _spectral_mix_dense(g_ref, ph_ref, yh_ref, nb=nb, blk=blk)
    for a in range(nb):
        y = jnp.dot(inv_ref[...], yh_ref[a], preferred_element_type=F32)
        rows = slice(a * blk, (a + 1) * blk)
        y = y + p_ref[rows, :] * skip_ref[...]
        o_ref[0, rows, :] = (y * x0_ref[rows, :]).astype(o_ref.dtype)


def _hy_mix(xn, w_in, b_in, w_short, b_short, gspec, skip, fwd, inv, *, tn, nb, blk):
    b, seq, d = xn.shape
    nt = d // tn

    def col(k):
        return lambda j, i: (0, k * nt + j)

    w_specs = [_resident((d, tn), col(k)) for k in range(3)]
    bi_specs = [pl.BlockSpec((1, tn), col(k)) for k in range(3)]
    ws_specs = [pl.BlockSpec((w_short.shape[0], tn), col(k)) for k in range(3)]
    bs_specs = [pl.BlockSpec((1, tn), col(k)) for k in range(3)]
    return pl.pallas_call(
        functools.partial(_hy_mix_kernel, nb=nb, blk=blk),
        grid=(nt, b),
        in_specs=[pl.BlockSpec((1, seq, d), lambda j, i: (i, 0, 0))]
        + w_specs + bi_specs + ws_specs + bs_specs
        + [_resident((2 * nb - 1, 2 * blk, tn), lambda j, i: (0, 0, j)),
           pl.BlockSpec((1, tn), lambda j, i: (0, j)),
           _resident(fwd.shape, lambda j, i: (0, 0)),
           _resident(inv.shape, lambda j, i: (0, 0))],
        out_specs=pl.BlockSpec((1, seq, tn), lambda j, i: (i, 0, j)),
        out_shape=jax.ShapeDtypeStruct((b, seq, d), BF16),
        scratch_shapes=[pltpu.VMEM((3, seq, tn), F32), pltpu.VMEM((seq, tn), F32), pltpu.VMEM((seq, tn), F32),
                        pltpu.VMEM((nb, 2 * blk, tn), F32), pltpu.VMEM((nb, 2 * blk, tn), BF16)],
        compiler_params=pltpu.CompilerParams(
            dimension_semantics=("arbitrary", "arbitrary"),
            vmem_limit_bytes=_vmem_limit(V7X_VMEM_BYTES)),
        name="hy_mix",
    )(xn, w_in, w_in, w_in, b_in, b_in, b_in, w_short, w_short, w_short, b_short, b_short, b_short,
      gspec, skip, fwd, inv)


def _filter_kernel(z_ref, w1_ref, b1_ref, f1_ref, w2_ref, b2_ref, f2_ref, w3_ref, b3_ref, f3_ref,
                   w4f_ref, w4b_ref, df_ref, fwd_ref, eneg_ref, g_ref, hf_ref, *, nb, blk):
    seq = z_ref.shape[0]

    @pl.when(pl.program_id(0) == 0)
    def _():
        hf = jnp.sin(f1_ref[...] * (jnp.dot(z_ref[...], w1_ref[...], preferred_element_type=F32) + b1_ref[...]))
        hf = jnp.sin(f2_ref[...] * (jnp.dot(hf, w2_ref[...], preferred_element_type=F32) + b2_ref[...]))
        hf_ref[...] = jnp.sin(f3_ref[...] * (jnp.dot(hf, w3_ref[...], preferred_element_type=F32) + b3_ref[...]))

    hf = hf_ref[...]
    row = lax.broadcasted_iota(jnp.int32, (seq, 1), 0)
    t = row.astype(F32) * (1.0 / (seq - 1))
    decay = jnp.exp(-t * df_ref[...])
    kf = jnp.dot(hf, w4f_ref[...], preferred_element_type=F32) * decay
    kb = jnp.dot(hf, w4b_ref[...], preferred_element_type=F32) * decay
    kb = jnp.where(row == 0, 0.0, kb)

    fwd = fwd_ref[...]
    eneg = eneg_ref[...]

    def spectra(k):
        pos, neg = [], []
        for jb in range(nb):
            kj = k[jb * blk:(jb + 1) * blk, :].astype(BF16)
            pos.append(jnp.dot(fwd, kj, preferred_element_type=F32))
            neg.append(jnp.dot(eneg, kj, preferred_element_type=F32) if jb < nb - 1 else None)
        return pos, neg

    fpos, fneg = spectra(kf)
    bpos, bneg = spectra(kb)

    def store(idx, val, conj_add=None):
        re, im = val[:blk], val[blk:]
        if conj_add is not None:
            re, im = re + conj_add[:blk], im - conj_add[blk:]
        g_ref[idx, 0:blk, :] = re
        g_ref[idx, blk:, :] = im

    store(nb - 1, fpos[0], conj_add=bpos[0])
    for c in range(1, nb):
        store(nb - 1 + c, fneg[c - 1] + fpos[c])
        gb = bneg[c - 1] + bpos[c]
        g_ref[nb - 1 - c, 0:blk, :] = gb[:blk]
        g_ref[nb - 1 - c, blk:, :] = -gb[blk:]


def _filter_spectra(z, w1, b1, f1, w2, b2, f2, w3, b3, f3, w4, deltas, fwd, eneg, *, td, nb, blk):
    seq = z.shape[0]
    d = deltas.shape[1]
    nt = d // td
    order = w2.shape[0]
    full = lambda a: _resident(a.shape, lambda j: (0,) * a.ndim)
    return pl.pallas_call(
        functools.partial(_filter_kernel, nb=nb, blk=blk),
        grid=(nt,),
        in_specs=[full(z), full(w1), full(b1), full(f1), full(w2), full(b2), full(f2), full(w3), full(b3), full(f3),
                  pl.BlockSpec((order, td), lambda j: (0, j)),
                  pl.BlockSpec((order, td), lambda j: (0, nt + j)),
                  pl.BlockSpec((1, td), lambda j: (0, j)),
                  full(fwd), full(eneg)],
        out_specs=pl.BlockSpec((2 * nb - 1, 2 * blk, td), lambda j: (0, 0, j)),
        out_shape=jax.ShapeDtypeStruct((2 * nb - 1, 2 * blk, d), F32),
        scratch_shapes=[pltpu.VMEM((seq, order), F32)],
        compiler_params=pltpu.CompilerParams(
            dimension_semantics=("arbitrary",),
            vmem_limit_bytes=_vmem_limit(2 * (2 * nb - 1) * 2 * blk * td * 4 + (4 * nb) * 2 * blk * td * 4
                                         + 6 * seq * td * 4 + (8 << 20))),
        name="hy_filter",
    )(z, w1, b1, f1, w2, b2, f2, w3, b3, f3, w4, w4, deltas, fwd, eneg)


def _proj_res_kernel(a_ref, x_ref, w_ref, b_ref, o_ref):
    o_ref[...] = x_ref[...] + jnp.dot(a_ref[...], w_ref[...], preferred_element_type=F32) + b_ref[...]


def _proj_res(a, x, w, b, *, tm):
    m, d = x.shape
    return pl.pallas_call(
        _proj_res_kernel,
        grid=(m // tm,),
        in_specs=[pl.BlockSpec((tm, d), lambda i: (i, 0)),
                  pl.BlockSpec((tm, d), lambda i: (i, 0)),
                  _resident((d, d), lambda i: (0, 0)),
                  _resident((1, d), lambda i: (0, 0))],
        out_specs=pl.BlockSpec((tm, d), lambda i: (i, 0)),
        out_shape=jax.ShapeDtypeStruct((m, d), F32),
        compiler_params=pltpu.CompilerParams(
            dimension_semantics=("arbitrary",),
            vmem_limit_bytes=_vmem_limit(d * d * 2 + 7 * tm * d * 4 + (4 << 20))),
        name="hy_out",
    )(a, x, w, b)


def _dft_angles(blk):
    f = (np.arange(blk, dtype=np.float64) + 0.5)[:, None]
    return f * (2.0 * np.pi / (2 * blk))


def _dft_tables(blk):
    w = _dft_angles(blk)
    q = np.arange(blk, dtype=np.float64)[None, :]
    ang = w * q
    fwd = np.concatenate([np.cos(ang), -np.sin(ang)], axis=0)
    ang_n = w * (q - blk)
    eneg = np.concatenate([np.cos(ang_n), -np.sin(ang_n)], axis=0)
    eneg[:, 0] = 0.0
    inv = np.concatenate([np.cos(ang).T, -np.sin(ang).T], axis=1) / blk
    return (jnp.asarray(fwd, dtype=BF16), jnp.asarray(eneg, dtype=BF16), jnp.asarray(inv, dtype=BF16))


def _dw_tables(blk, width, taps_pad):
    w = _dft_angles(blk)
    pad = (width - 1) // 2
    tab = np.zeros((3, 2 * blk, taps_pad), dtype=np.float64)
    for ci, c in enumerate((-1, 0, 1)):
        for j in range(width):
            e = pad - j - blk * c
            if abs(e) <= blk - 1:
                tab[ci, :blk, j] = np.cos(w[:, 0] * e)
                tab[ci, blk:, j] = -np.sin(w[:, 0] * e)
    return jnp.asarray(tab, dtype=F32)


def _hyena_positional_features(seq, pad_to):
    t = jnp.linspace(0.0, 1.0, seq, dtype=F32)[:, None]
    bands = jnp.linspace(1e-4, HYENA_N_BANDS - 1, HYENA_N_BANDS, dtype=F32)[None, :]
    wpos = (2.0 * math.pi) * jnp.arange(seq, dtype=F32)[:, None] / seq
    z = jnp.concatenate([t, jnp.cos(bands * wpos), -jnp.sin(bands * wpos)], axis=-1)
    return jnp.pad(z, ((0, 0), (0, pad_to - z.shape[1])))


def _hyena_decay_rates(d):
    max_decay = math.log(HYENA_DECAY_TARGET) / HYENA_FAST_DECAY_PCT
    min_decay = math.log(HYENA_DECAY_TARGET) / HYENA_SLOW_DECAY_PCT
    return jnp.abs(jnp.linspace(min_decay, max_decay, d, dtype=F32))[None, :]


def _pick(n, prefs):
    for p in prefs:
        if n % p == 0:
            return p
    return n


def kernel(x, norm_mix, norm_ffn, cv_w_pw1, cv_b_pw1, cv_w_dw, cv_b_dw, cv_ln_g, cv_ln_b, cv_w_pw2, cv_b_pw2, hy_w_in, hy_b_in, hy_w_short, hy_b_short, hy_f_w1, hy_f_b1, hy_f_freq1, hy_f_w2, hy_f_b2, hy_f_freq2, hy_f_w3, hy_f_b3, hy_f_freq3, hy_f_w4, hy_skip, hy_w_out, hy_b_out, ffn_w_gate, ffn_w_up, ffn_w_down, norm_final):
    bsz, seq, d = x.shape
    m = bsz * seq
    f = ffn_w_gate.shape[-1]
    depth = norm_mix.shape[0]
    assert depth == 2 and cv_w_pw1.shape[0] == 1 and hy_w_in.shape[0] == 1

    tm = _pick(m, (512, 256, 128))
    tf = _pick(f, (512, 256, 128))
    tn_col = _pick(d, (256, 128))
    td = _pick(d, (512, 256, 128))
    blk = _pick(seq, (CONV_BLOCK, 256, 128))
    nb = seq // blk
    dw_blk = _pick(seq, (DW_BLOCK, 128))
    width = cv_w_dw.shape[1]
    assert (width - 1) // 2 < dw_blk

    row = lambda a: a.reshape(1, -1).astype(F32)
    bf = lambda a: a.astype(BF16)

    h = x.reshape(m, d)

    taps_pad = -(-width // V7X_SUBLANES) * V7X_SUBLANES
    wdw = jnp.pad(cv_w_dw[0], ((0, taps_pad - width), (0, 0)))
    dw_fwd, _, dw_inv = _dft_tables(dw_blk)
    dw_spec = _dw_spectra(_dw_tables(dw_blk, width, taps_pad), wdw, td=td)
    y = _conf_in(x, row(norm_mix[0]), bf(cv_w_pw1[0]), row(cv_b_pw1[0]), dw_spec, row(cv_b_dw[0]),
                 dw_fwd, dw_inv, tn=tn_col, blk=dw_blk)
    h = _conf_out(y.reshape(m, d), h, row(cv_ln_g[0]), row(cv_ln_b[0]), bf(cv_w_pw2[0]), row(cv_b_pw2[0]), tm=tm)
    wg, wu, wd = bf(ffn_w_gate), bf(ffn_w_up), bf(ffn_w_down)
    h, hn = _ffn(h, row(norm_ffn[0]), wg, wu, wd, row(norm_mix[1]), layer=0, tm=tm, tf=tf, final_norm=False)

    fwd, eneg, inv = _dft_tables(blk)
    emb = hy_f_w1.shape[1]
    emb_pad = -(-emb // V7X_LANES) * V7X_LANES
    z = _hyena_positional_features(seq, emb_pad)
    w1 = jnp.pad(hy_f_w1[0], ((0, emb_pad - emb), (0, 0)))
    gspec = _filter_spectra(z, w1, row(hy_f_b1[0]), row(hy_f_freq1[0]), hy_f_w2[0], row(hy_f_b2[0]),
                            row(hy_f_freq2[0]), hy_f_w3[0], row(hy_f_b3[0]), row(hy_f_freq3[0]), hy_f_w4[0],
                            _hyena_decay_rates(d), fwd, eneg, td=td, nb=nb, blk=blk)
    gated = _hy_mix(hn.reshape(bsz, seq, d), bf(hy_w_in[0]), row(hy_b_in[0]), hy_w_short[0], row(hy_b_short[0]),
                    gspec, row(hy_skip[0]), fwd, inv, tn=tn_col, nb=nb, blk=blk)
    h = _proj_res(gated.reshape(m, d), h, bf(hy_w_out[0]), row(hy_b_out[0]), tm=tm)
    h = _ffn(h, row(norm_ffn[1]), wg, wu, wd, row(norm_final), layer=1, tm=tm, tf=tf, final_norm=True)
    return h.reshape(bsz, seq, d)
```

```python
import functools
import math

import jax
import jax.numpy as jnp
import numpy as np
from jax import lax
from jax.experimental import pallas as pl
from jax.experimental.pallas import tpu as pltpu

NORM_EPS = 1e-6
LN_EPS = 1e-5
HYENA_N_BANDS = 16
HYENA_FAST_DECAY_PCT = 0.3
HYENA_SLOW_DECAY_PCT = 1.5
HYENA_DECAY_TARGET = 1e-2

V7X_LANES = 128
V7X_SUBLANES = 8
V7X_BF16_SUBLANES = 16
V7X_VMEM_BYTES = 64 * 1024 * 1024

DW_BLOCK = 256
CONV_BLOCK = 512

F32 = jnp.float32
BF16 = jnp.bfloat16


def _vmem_limit(nbytes):
    return int(min(max(nbytes, 32 * 1024 * 1024), V7X_VMEM_BYTES - 4 * 1024 * 1024))


def _resident(block_shape, index_map):
    return pl.BlockSpec(block_shape, index_map, pipeline_mode=pl.Buffered(1))


def _rmsnorm_bf16(x, g):
    ms = jnp.mean(x * x, axis=-1, keepdims=True)
    return (x * lax.rsqrt(ms + NORM_EPS) * g).astype(BF16)


def _spectral_mix(g_ref, ph_ref, yh_ref, a, *, nb, blk, lag_blocks):
    lanes = yh_ref.shape[-1]
    slab = V7X_BF16_SUBLANES
    lane_w = min(lanes, 2 * V7X_LANES)
    for s in range(blk // slab):
        r_re = slice(s * slab, (s + 1) * slab)
        r_im = slice(blk + s * slab, blk + (s + 1) * slab)
        for lc in range(lanes // lane_w):
            ls = slice(lc * lane_w, (lc + 1) * lane_w)
            acc_re = acc_im = None
            for b in range(max(0, a - lag_blocks), min(nb, a + lag_blocks + 1)):
                gi = a - b + lag_blocks
                gre, gim = g_ref[gi, r_re, ls], g_ref[gi, r_im, ls]
                pre, pim = ph_ref[b, r_re, ls], ph_ref[b, r_im, ls]
                t_re = gre * pre - gim * pim
                t_im = gre * pim + gim * pre
                acc_re = t_re if acc_re is None else acc_re + t_re
                acc_im = t_im if acc_im is None else acc_im + t_im
            yh_ref[r_re, ls] = acc_re.astype(yh_ref.dtype)
            yh_ref[r_im, ls] = acc_im.astype(yh_ref.dtype)


def _dw_spectra_kernel(e_ref, w_ref, o_ref):
    for c in range(e_ref.shape[0]):
        o_ref[c] = jnp.dot(e_ref[c], w_ref[...], preferred_element_type=F32)


def _dw_spectra(etab, wdw, *, td):
    nc, rows, taps = etab.shape
    d = wdw.shape[1]
    return pl.pallas_call(
        _dw_spectra_kernel,
        grid=(d // td,),
        in_specs=[_resident(etab.shape, lambda j: (0, 0, 0)), pl.BlockSpec((taps, td), lambda j: (0, j))],
        out_specs=pl.BlockSpec((nc, rows, td), lambda j: (0, 0, j)),
        out_shape=jax.ShapeDtypeStruct((nc, rows, d), F32),
        compiler_params=pltpu.CompilerParams(dimension_semantics=("arbitrary",)),
        name="dw_spectra",
    )(etab, wdw)


def _conf_in_kernel(x_ref, g_ref, wa_ref, wb_ref, ba_ref, bb_ref, gs_ref, bdw_ref, fwd_ref, inv_ref,
                    y_ref, xn_ref, u_ref, ph_ref, yh_ref, *, nb, blk):
    @pl.when(pl.program_id(1) == 0)
    def _():
        xn_ref[...] = _rmsnorm_bf16(x_ref[0], g_ref[...])

    rc = min(CONV_BLOCK, u_ref.shape[0])
    for c in range(u_ref.shape[0] // rc):
        rows = slice(c * rc, (c + 1) * rc)
        a1 = jnp.dot(xn_ref[rows, :], wa_ref[...], preferred_element_type=F32) + ba_ref[...]
        a2 = jnp.dot(xn_ref[rows, :], wb_ref[...], preferred_element_type=F32) + bb_ref[...]
        u_ref[rows, :] = (a1 * jax.nn.sigmoid(a2)).astype(BF16)
    for b in range(nb):
        ph_ref[b] = jnp.dot(fwd_ref[...], u_ref[b * blk:(b + 1) * blk, :], preferred_element_type=F32)
    for a in range(nb):
        _spectral_mix(gs_ref, ph_ref, yh_ref, a, nb=nb, blk=blk, lag_blocks=1)
        y = jnp.dot(inv_ref[...], yh_ref[...], preferred_element_type=F32)
        y_ref[0, a * blk:(a + 1) * blk, :] = y + bdw_ref[...]


def _conf_in(x, g, w, b, gs, bdw, fwd, inv, *, tn, blk):
    bsz, seq, d = x.shape
    nt = d // tn
    nb = seq // blk
    return pl.pallas_call(
        functools.partial(_conf_in_kernel, nb=nb, blk=blk),
        grid=(bsz, nt),
        in_specs=[
            _resident((1, seq, d), lambda i, j: (i, 0, 0)),
            _resident((1, d), lambda i, j: (0, 0)),
            pl.BlockSpec((d, tn), lambda i, j: (0, j)),
            pl.BlockSpec((d, tn), lambda i, j: (0, nt + j)),
            pl.BlockSpec((1, tn), lambda i, j: (0, j)),
            pl.BlockSpec((1, tn), lambda i, j: (0, nt + j)),
            pl.BlockSpec((gs.shape[0], 2 * blk, tn), lambda i, j: (0, 0, j)),
            pl.BlockSpec((1, tn), lambda i, j: (0, j)),
            _resident(fwd.shape, lambda i, j: (0, 0)),
            _resident(inv.shape, lambda i, j: (0, 0)),
        ],
        out_specs=pl.BlockSpec((1, seq, tn), lambda i, j: (i, 0, j)),
        out_shape=jax.ShapeDtypeStruct((bsz, seq, d), F32),
        scratch_shapes=[pltpu.VMEM((seq, d), BF16), pltpu.VMEM((seq, tn), BF16),
                        pltpu.VMEM((nb, 2 * blk, tn), F32), pltpu.VMEM((2 * blk, tn), BF16)],
        compiler_params=pltpu.CompilerParams(
            dimension_semantics=("arbitrary", "arbitrary"),
            vmem_limit_bytes=_vmem_limit(seq * d * 6 + 8 * d * tn + 14 * seq * tn * 4 + (6 << 20))),
        name="conf_in",
    )(x, g, w, w, b, b, gs, bdw, fwd, inv)


def _conf_out_kernel(y_ref, x_ref, lg_ref, lb_ref, w2_ref, b2_ref, o_ref):
    y = y_ref[...]
    mu = jnp.mean(y, axis=-1, keepdims=True)
    yc = y - mu
    var = jnp.mean(yc * yc, axis=-1, keepdims=True)
    z = yc * lax.rsqrt(var + LN_EPS) * lg_ref[...] + lb_ref[...]
    act = (z * jax.nn.sigmoid(z)).astype(BF16)
    o_ref[...] = x_ref[...] + jnp.dot(act, w2_ref[...], preferred_element_type=F32) + b2_ref[...]


def _conf_out(y, x, lg, lb, w2, b2, *, tm):
    m, d = y.shape
    return pl.pallas_call(
        _conf_out_kernel,
        grid=(m // tm,),
        in_specs=[
            pl.BlockSpec((tm, d), lambda i: (i, 0)),
            pl.BlockSpec((tm, d), lambda i: (i, 0)),
            _resident((1, d), lambda i: (0, 0)),
            _resident((1, d), lambda i: (0, 0)),
            _resident((d, d), lambda i: (0, 0)),
            _resident((1, d), lambda i: (0, 0)),
        ],
        out_specs=pl.BlockSpec((tm, d), lambda i: (i, 0)),
        out_shape=jax.ShapeDtypeStruct((m, d), F32),
        compiler_params=pltpu.CompilerParams(
            dimension_semantics=("arbitrary",),
            vmem_limit_bytes=_vmem_limit(d * d * 2 + 12 * tm * d * 4 + (4 << 20))),
        name="conf_out",
    )(y, x, lg, lb, w2, b2)


def _ffn_kernel(x_ref, g_ref, wg_ref, wu_ref, wd_ref, gn_ref, o_ref, *rest, final_norm):
    xn_ref = rest[-1]
    j = pl.program_id(1)

    @pl.when(j == 0)
    def _():
        x = x_ref[...]
        xn_ref[...] = _rmsnorm_bf16(x, g_ref[...])
        o_ref[...] = x

    xn = xn_ref[...]
    gate = jnp.dot(xn, wg_ref[...], preferred_element_type=F32)
    up = jnp.dot(xn, wu_ref[...], preferred_element_type=F32)
    act = (gate * jax.nn.sigmoid(gate) * up).astype(BF16)
    o_ref[...] += jnp.dot(act, wd_ref[...], preferred_element_type=F32)

    @pl.when(j == pl.num_programs(1) - 1)
    def _():
        h = o_ref[...]
        ms = jnp.mean(h * h, axis=-1, keepdims=True)
        hn = h * lax.rsqrt(ms + NORM_EPS) * gn_ref[...]
        if final_norm:
            o_ref[...] = hn
        else:
            rest[0][...] = hn.astype(BF16)


def _ffn(x, g, wg, wu, wd, gn, *, layer, tm, tf, final_norm):
    m, d = x.shape
    f = wg.shape[-1]
    row_spec = pl.BlockSpec((tm, d), lambda i, j: (i, 0))
    out_specs, out_shape = row_spec, jax.ShapeDtypeStruct((m, d), F32)
    if not final_norm:
        out_specs = [row_spec, row_spec]
        out_shape = [out_shape, jax.ShapeDtypeStruct((m, d), BF16)]
    return pl.pallas_call(
        functools.partial(_ffn_kernel, final_norm=final_norm),
        grid=(m // tm, f // tf),
        in_specs=[
            row_spec,
            _resident((1, d), lambda i, j: (0, 0)),
            pl.BlockSpec((None, d, tf), lambda i, j: (layer, 0, j)),
            pl.BlockSpec((None, d, tf), lambda i, j: (layer, 0, j)),
            pl.BlockSpec((None, tf, d), lambda i, j: (layer, j, 0)),
            _resident((1, d), lambda i, j: (0, 0)),
        ],
        out_specs=out_specs,
        out_shape=out_shape,
        scratch_shapes=[pltpu.VMEM((tm, d), BF16)],
        compiler_params=pltpu.CompilerParams(
            dimension_semantics=("arbitrary", "arbitrary"),
            vmem_limit_bytes=_vmem_limit(5 * tm * d * 4 + 3 * tm * d * 2 + 12 * d * tf + 5 * tm * tf * 4 + (4 << 20))),
        name="ffn_final" if final_norm else "ffn",
    )(x, g, wg, wu, wd, gn)


def _hy_mix_kernel(xn_ref, w0_ref, w1_ref, w2_ref, bi0_ref, bi1_ref, bi2_ref,
                   ws0_ref, ws1_ref, ws2_ref, bs0_ref, bs1_ref, bs2_ref,
                   g_ref, skip_ref, fwd_ref, inv_ref, o_ref, z_ref, p_ref, x0_ref, ph_ref, yh_ref, *, nb, blk):
    seq = xn_ref.shape[1]
    halo = V7X_SUBLANES
    branches = ((w0_ref, bi0_ref, ws0_ref, bs0_ref), (w1_ref, bi1_ref, ws1_ref, bs1_ref),
                (w2_ref, bi2_ref, ws2_ref, bs2_ref))

    def short_conv(k):
        lo, hi = max(k * blk - halo, 0), min((k + 1) * blk + halo, seq)
        n = hi - lo
        off = k * blk - lo
        row = lax.broadcasted_iota(jnp.int32, (n, 1), 0)
        outs = []
        for br, (_, _, ws_ref, bs_ref) in enumerate(branches):
            z = z_ref[br, lo:hi, :]
            zp = pltpu.roll(z, 1, axis=0)
            zn = pltpu.roll(z, n - 1, axis=0)
            if lo == 0:
                zp = jnp.where(row == 0, 0.0, zp)
            if hi == seq:
                zn = jnp.where(row == n - 1, 0.0, zn)
            y = ws_ref[0:1, :] * zp + ws_ref[1:2, :] * z + ws_ref[2:3, :] * zn + bs_ref[...]
            outs.append(y[off:off + blk, :])
        x0_ref[k * blk:(k + 1) * blk, :] = outs[0]
        p_ref[k * blk:(k + 1) * blk, :] = outs[2] * outs[1]

    per = 2 if nb % 2 == 0 else 1
    for c in range(nb // per):
        rows = slice(c * per * blk, (c + 1) * per * blk)
        for br, (w_ref, bi_ref, _, _) in enumerate(branches):
            z_ref[br, rows, :] = jnp.dot(xn_ref[0, rows, :], w_ref[...], preferred_element_type=F32) + bi_ref[...]
        for k in range(c * per - 1, (c + 1) * per - 1):
            if k >= 0:
                short_conv(k)
    short_conv(nb - 1)

    for b in range(nb):
        pb = p_ref[b * blk:(b + 1) * blk, :].astype(BF16)
        ph_ref[b] = jnp.dot(fwd_ref[...], pb, preferred_element_type=F32)
    for a in range(nb):
        _spectral_mix(g_ref, ph_ref, yh_ref, a, nb=nb, blk=blk, lag_blocks=nb - 1)
        y = jnp.dot(inv_ref[...], yh_ref[...], preferred_element_type=F32)
        rows = slice(a * blk, (a + 1) * blk)
        y = y + p_ref[rows, :] * skip_ref[...]
        o_ref[0, rows, :] = (y * x0_ref[rows, :]).astype(o_ref.dtype)


def _hy_mix(xn, w_in, b_in, w_short, b_short, gspec, skip, fwd, inv, *, tn, nb, blk):
    b, seq, d = xn.shape
    nt = d // tn

    def col(k):
        return lambda j, i: (0, k * nt + j)

    w_specs = [_resident((d, tn), col(k)) for k in range(3)]
    bi_specs = [pl.BlockSpec((1, tn), col(k)) for k in range(3)]
    ws_specs = [pl.BlockSpec((w_short.shape[0], tn), col(k)) for k in range(3)]
    bs_specs = [pl.BlockSpec((1, tn), col(k)) for k in range(3)]
    return pl.pallas_call(
        functools.partial(_hy_mix_kernel, nb=nb, blk=blk),
        grid=(nt, b),
        in_specs=[pl.BlockSpec((1, seq, d), lambda j, i: (i, 0, 0))]
        + w_specs + bi_specs + ws_specs + bs_specs
        + [_resident((2 * nb - 1, 2 * blk, tn), lambda j, i: (0, 0, j)),
           pl.BlockSpec((1, tn), lambda j, i: (0, j)),
           _resident(fwd.shape, lambda j, i: (0, 0)),
           _resident(inv.shape, lambda j, i: (0, 0))],
        out_specs=pl.BlockSpec((1, seq, tn), lambda j, i: (i, 0, j)),
        out_shape=jax.ShapeDtypeStruct((b, seq, d), BF16),
        scratch_shapes=[pltpu.VMEM((3, seq, tn), F32), pltpu.VMEM((seq, tn), F32), pltpu.VMEM((seq, tn), F32),
                        pltpu.VMEM((nb, 2 * blk, tn), F32), pltpu.VMEM((2 * blk, tn), BF16)],
        compiler_params=pltpu.CompilerParams(
            dimension_semantics=("arbitrary", "arbitrary"),
            vmem_limit_bytes=_vmem_limit(V7X_VMEM_BYTES)),
        name="hy_mix",
    )(xn, w_in, w_in, w_in, b_in, b_in, b_in, w_short, w_short, w_short, b_short, b_short, b_short,
      gspec, skip, fwd, inv)


def _filter_kernel(z_ref, w1_ref, b1_ref, f1_ref, w2_ref, b2_ref, f2_ref, w3_ref, b3_ref, f3_ref,
                   w4f_ref, w4b_ref, df_ref, fwd_ref, eneg_ref, g_ref, hf_ref, *, nb, blk):
    seq = z_ref.shape[0]

    @pl.when(pl.program_id(0) == 0)
    def _():
        hf = jnp.sin(f1_ref[...] * (jnp.dot(z_ref[...], w1_ref[...], preferred_element_type=F32) + b1_ref[...]))
        hf = jnp.sin(f2_ref[...] * (jnp.dot(hf, w2_ref[...], preferred_element_type=F32) + b2_ref[...]))
        hf_ref[...] = jnp.sin(f3_ref[...] * (jnp.dot(hf, w3_ref[...], preferred_element_type=F32) + b3_ref[...]))

    hf = hf_ref[...]
    row = lax.broadcasted_iota(jnp.int32, (seq, 1), 0)
    t = row.astype(F32) * (1.0 / (seq - 1))
    decay = jnp.exp(-t * df_ref[...])
    kf = jnp.dot(hf, w4f_ref[...], preferred_element_type=F32) * decay
    kb = jnp.dot(hf, w4b_ref[...], preferred_element_type=F32) * decay
    kb = jnp.where(row == 0, 0.0, kb)

    fwd = fwd_ref[...]
    eneg = eneg_ref[...]

    def spectra(k):
        pos, neg = [], []
        for jb in range(nb):
            kj = k[jb * blk:(jb + 1) * blk, :].astype(BF16)
            pos.append(jnp.dot(fwd, kj, preferred_element_type=F32))
            neg.append(jnp.dot(eneg, kj, preferred_element_type=F32) if jb < nb - 1 else None)
        return pos, neg

    fpos, fneg = spectra(kf)
    bpos, bneg = spectra(kb)

    def store(idx, val, conj_add=None):
        re, im = val[:blk], val[blk:]
        if conj_add is not None:
            re, im = re + conj_add[:blk], im - conj_add[blk:]
        g_ref[idx, 0:blk, :] = re
        g_ref[idx, blk:, :] = im

    store(nb - 1, fpos[0], conj_add=bpos[0])
    for c in range(1, nb):
        store(nb - 1 + c, fneg[c - 1] + fpos[c])
        gb = bneg[c - 1] + bpos[c]
        g_ref[nb - 1 - c, 0:blk, :] = gb[:blk]
        g_ref[nb - 1 - c, blk:, :] = -gb[blk:]


def _filter_spectra(z, w1, b1, f1, w2, b2, f2, w3, b3, f3, w4, deltas, fwd, eneg, *, td, nb, blk):
    seq = z.shape[0]
    d = deltas.shape[1]
    nt = d // td
    order = w2.shape[0]
    full = lambda a: _resident(a.shape, lambda j: (0,) * a.ndim)
    return pl.pallas_call(
        functools.partial(_filter_kernel, nb=nb, blk=blk),
        grid=(nt,),
        in_specs=[full(z), full(w1), full(b1), full(f1), full(w2), full(b2), full(f2), full(w3), full(b3), full(f3),
                  pl.BlockSpec((order, td), lambda j: (0, j)),
                  pl.BlockSpec((order, td), lambda j: (0, nt + j)),
                  pl.BlockSpec((1, td), lambda j: (0, j)),
                  full(fwd), full(eneg)],
        out_specs=pl.BlockSpec((2 * nb - 1, 2 * blk, td), lambda j: (0, 0, j)),
        out_shape=jax.ShapeDtypeStruct((2 * nb - 1, 2 * blk, d), F32),
        scratch_shapes=[pltpu.VMEM((seq, order), F32)],
        compiler_params=pltpu.CompilerParams(
            dimension_semantics=("arbitrary",),
            vmem_limit_bytes=_vmem_limit(2 * (2 * nb - 1) * 2 * blk * td * 4 + (4 * nb) * 2 * blk * td * 4
                                         + 6 * seq * td * 4 + (8 << 20))),
        name="hy_filter",
    )(z, w1, b1, f1, w2, b2, f2, w3, b3, f3, w4, w4, deltas, fwd, eneg)


def _proj_res_kernel(a_ref, x_ref, w_ref, b_ref, o_ref):
    o_ref[...] = x_ref[...] + jnp.dot(a_ref[...], w_ref[...], preferred_element_type=F32) + b_ref[...]


def _proj_res(a, x, w, b, *, tm):
    m, d = x.shape
    return pl.pallas_call(
        _proj_res_kernel,
        grid=(m // tm,),
        in_specs=[pl.BlockSpec((tm, d), lambda i: (i, 0)),
                  pl.BlockSpec((tm, d), lambda i: (i, 0)),
                  _resident((d, d), lambda i: (0, 0)),
                  _resident((1, d), lambda i: (0, 0))],
        out_specs=pl.BlockSpec((tm, d), lambda i: (i, 0)),
        out_shape=jax.ShapeDtypeStruct((m, d), F32),
        compiler_params=pltpu.CompilerParams(
            dimension_semantics=("arbitrary",),
            vmem_limit_bytes=_vmem_limit(d * d * 2 + 7 * tm * d * 4 + (4 << 20))),
        name="hy_out",
    )(a, x, w, b)


def _dft_angles(blk):
    f = (np.arange(blk, dtype=np.float64) + 0.5)[:, None]
    return f * (2.0 * np.pi / (2 * blk))


def _dft_tables(blk):
    w = _dft_angles(blk)
    q = np.arange(blk, dtype=np.float64)[None, :]
    ang = w * q
    fwd = np.concatenate([np.cos(ang), -np.sin(ang)], axis=0)
    ang_n = w * (q - blk)
    eneg = np.concatenate([np.cos(ang_n), -np.sin(ang_n)], axis=0)
    eneg[:, 0] = 0.0
    inv = np.concatenate([np.cos(ang).T, -np.sin(ang).T], axis=1) / blk
    return (jnp.asarray(fwd, dtype=BF16), jnp.asarray(eneg, dtype=BF16), jnp.asarray(inv, dtype=BF16))


def _dw_tables(blk, width, taps_pad):
    w = _dft_angles(blk)
    pad = (width - 1) // 2
    tab = np.zeros((3, 2 * blk, taps_pad), dtype=np.float64)
    for ci, c in enumerate((-1, 0, 1)):
        for j in range(width):
            e = pad - j - blk * c
            if abs(e) <= blk - 1:
                tab[ci, :blk, j] = np.cos(w[:, 0] * e)
                tab[ci, blk:, j] = -np.sin(w[:, 0] * e)
    return jnp.asarray(tab, dtype=F32)


def _hyena_positional_features(seq, pad_to):
    t = jnp.linspace(0.0, 1.0, seq, dtype=F32)[:, None]
    bands = jnp.linspace(1e-4, HYENA_N_BANDS - 1, HYENA_N_BANDS, dtype=F32)[None, :]
    wpos = (2.0 * math.pi) * jnp.arange(seq, dtype=F32)[:, None] / seq
    z = jnp.concatenate([t, jnp.cos(bands * wpos), -jnp.sin(bands * wpos)], axis=-1)
    return jnp.pad(z, ((0, 0), (0, pad_to - z.shape[1])))


def _hyena_decay_rates(d):
    max_decay = math.log(HYENA_DECAY_TARGET) / HYENA_FAST_DECAY_PCT
    min_decay = math.log(HYENA_DECAY_TARGET) / HYENA_SLOW_DECAY_PCT
    return jnp.abs(jnp.linspace(min_decay, max_decay, d, dtype=F32))[None, :]


def _pick(n, prefs):
    for p in prefs:
        if n % p == 0:
            return p
    return n


def kernel(x, norm_mix, norm_ffn, cv_w_pw1, cv_b_pw1, cv_w_dw, cv_b_dw, cv_ln_g, cv_ln_b, cv_w_pw2, cv_b_pw2, hy_w_in, hy_b_in, hy_w_short, hy_b_short, hy_f_w1, hy_f_b1, hy_f_freq1, hy_f_w2, hy_f_b2, hy_f_freq2, hy_f_w3, hy_f_b3, hy_f_freq3, hy_f_w4, hy_skip, hy_w_out, hy_b_out, ffn_w_gate, ffn_w_up, ffn_w_down, norm_final):
    bsz, seq, d = x.shape
    m = bsz * seq
    f = ffn_w_gate.shape[-1]
    depth = norm_mix.shape[0]
    assert depth == 2 and cv_w_pw1.shape[0] == 1 and hy_w_in.shape[0] == 1

    tm = _pick(m, (512, 256, 128))
    tf = _pick(f, (512, 256, 128))
    tn_col = _pick(d, (256, 128))
    td = _pick(d, (512, 256, 128))
    blk = _pick(seq, (CONV_BLOCK, 256, 128))
    nb = seq // blk
    dw_blk = _pick(seq, (DW_BLOCK, 128))
    width = cv_w_dw.shape[1]
    assert (width - 1) // 2 < dw_blk

    row = lambda a: a.reshape(1, -1).astype(F32)
    bf = lambda a: a.astype(BF16)

    h = x.reshape(m, d)

    taps_pad = -(-width // V7X_SUBLANES) * V7X_SUBLANES
    wdw = jnp.pad(cv_w_dw[0], ((0, taps_pad - width), (0, 0)))
    dw_fwd, _, dw_inv = _dft_tables(dw_blk)
    dw_spec = _dw_spectra(_dw_tables(dw_blk, width, taps_pad), wdw, td=td)
    y = _conf_in(x, row(norm_mix[0]), bf(cv_w_pw1[0]), row(cv_b_pw1[0]), dw_spec, row(cv_b_dw[0]),
                 dw_fwd, dw_inv, tn=tn_col, blk=dw_blk)
    h = _conf_out(y.reshape(m, d), h, row(cv_ln_g[0]), row(cv_ln_b[0]), bf(cv_w_pw2[0]), row(cv_b_pw2[0]), tm=tm)
    wg, wu, wd = bf(ffn_w_gate), bf(ffn_w_up), bf(ffn_w_down)
    h, hn = _ffn(h, row(norm_ffn[0]), wg, wu, wd, row(norm_mix[1]), layer=0, tm=tm, tf=tf, final_norm=False)

    fwd, eneg, inv = _dft_tables(blk)
    emb = hy_f_w1.shape[1]
    emb_pad = -(-emb // V7X_LANES) * V7X_LANES
    z = _hyena_positional_features(seq, emb_pad)
    w1 = jnp.pad(hy_f_w1[0], ((0, emb_pad - emb), (0, 0)))
    gspec = _filter_spectra(z, w1, row(hy_f_b1[0]), row(hy_f_freq1[0]), hy_f_w2[0], row(hy_f_b2[0]),
                            row(hy_f_freq2[0]), hy_f_w3[0], row(hy_f_b3[0]), row(hy_f_freq3[0]), hy_f_w4[0],
                            _hyena_decay_rates(d), fwd, eneg, td=td, nb=nb, blk=blk)
    gated = _hy_mix(hn.reshape(bsz, seq, d), bf(hy_w_in[0]), row(hy_b_in[0]), hy_w_short[0], row(hy_b_short[0]),
                    gspec, row(hy_skip[0]), fwd, inv, tn=tn_col, nb=nb, blk=blk)
    h = _proj_res(gated.reshape(m, d), h, bf(hy_w_out[0]), row(hy_b_out[0]), tm=tm)
    h = _ffn(h, row(norm_ffn[1]), wg, wu, wd, row(norm_final), layer=1, tm=tm, tf=tf, final_norm=True)
    return h.reshape(bsz, seq, d)
```
